```python
import jax
import jax.numpy as jnp
from jax import lax
import numpy as np


D_MODEL = 1024
BATCH = 8
SEQ = 4096
DEPTH = 1

CHUNK = 64
N_MEM = 256
D_MIX = D_MODEL
D_RNN = D_MIX // 2
D_ATT = D_MIX - D_RNN
N_RNN_BLOCKS = 8
RNN_BLOCK = D_RNN // N_RNN_BLOCKS
CONV_WIDTH = 4
RG_LRU_C = 8.0
N_ATT_HEADS = 8
ATT_HEAD_DIM = D_ATT // N_ATT_HEADS
LEFT_CHUNKS = 8
BAND = (LEFT_CHUNKS + 1) * CHUNK
REL_CLIP = 128
N_XHEADS = 4
XHEAD_DIM = D_MODEL // N_XHEADS
N_GROUPS = 4
EXPERTS_PER_GROUP = 8
N_EXPERTS = N_GROUPS * EXPERTS_PER_GROUP
TOP_K = 2
D_EXPERT = D_MODEL // 2
MOE_BLOCK = 128
D_IN_PROJ = 2 * D_RNN + 3 * D_ATT
EPS = 1e-6
NEG_INF = -1e30

kernel_name = 'hymba_style_streaming_hybrid_block'


def rms_norm(x, w):
    xf = x.astype(jnp.float32)
    y = xf * lax.rsqrt(jnp.mean(xf * xf, axis=-1, keepdims=True) + EPS)
    return (y * w.astype(jnp.float32)).astype(x.dtype)


def _linear_recurrence_combine(left, right):
    a1, b1 = left
    a2, b2 = right
    return a1 * a2, a2 * b1 + b2


def recurrent_group(xr, xg, conv_w, conv_b, w_a, b_a, w_x, b_x, lam):
    b, s, _ = xr.shape
    xc = lax.conv_general_dilated(
        xr, conv_w[:, None, :].astype(xr.dtype), window_strides=(1,),
        padding=[(CONV_WIDTH - 1, 0)], dimension_numbers=('NWC', 'WIO', 'NWC'),
        feature_group_count=D_RNN) + conv_b
    xb = xc.reshape(b, s, N_RNN_BLOCKS, RNN_BLOCK)
    r = jax.nn.sigmoid((jnp.einsum('bsnc,ncd->bsnd', xb, w_a).reshape(b, s, D_RNN) + b_a).astype(jnp.float32))
    i = jax.nn.sigmoid((jnp.einsum('bsnc,ncd->bsnd', xb, w_x).reshape(b, s, D_RNN) + b_x).astype(jnp.float32))
    log_a = -RG_LRU_C * r * jax.nn.softplus(-lam.astype(jnp.float32))
    a = jnp.exp(log_a)
    u = jnp.sqrt(-jnp.expm1(2.0 * log_a)) * (i * xc.astype(jnp.float32))
    _, h = lax.associative_scan(_linear_recurrence_combine, (a, u), axis=1)
    return (jax.nn.gelu(xg.astype(jnp.float32)) * h).astype(xr.dtype)


def rel_bias_band(rel_bias):
    qi = np.arange(CHUNK)[:, None]
    kj = np.arange(BAND)[None, :]
    dist = LEFT_CHUNKS * CHUNK + qi - kj
    idx = np.clip(dist, -REL_CLIP, REL_CLIP) + REL_CLIP
    return rel_bias[:, idx]


def chunk_band_attention(q, k, v, rel_bias):
    b, s, _ = q.shape
    nc = s // CHUNK
    qc = q.reshape(b, nc, CHUNK, N_ATT_HEADS, ATT_HEAD_DIM)
    kc = k.reshape(b, nc, CHUNK, N_ATT_HEADS, ATT_HEAD_DIM)
    vc = v.reshape(b, nc, CHUNK, N_ATT_HEADS, ATT_HEAD_DIM)
    pad = ((0, 0), (LEFT_CHUNKS, 0), (0, 0), (0, 0), (0, 0))
    kp = jnp.pad(kc, pad)
    vp = jnp.pad(vc, pad)
    kb = jnp.concatenate([kp[:, j:j + nc] for j in range(LEFT_CHUNKS + 1)], axis=2)
    vb = jnp.concatenate([vp[:, j:j + nc] for j in range(LEFT_CHUNKS + 1)], axis=2)
    chunk_id = np.arange(nc)[:, None] - LEFT_CHUNKS + np.arange(LEFT_CHUNKS + 1)[None, :]
    valid = np.repeat(chunk_id >= 0, CHUNK, axis=1)
    scores = jnp.einsum('bnqhd,bnkhd->bhnqk', qc, kb, preferred_element_type=jnp.float32)
    scores = scores * (ATT_HEAD_DIM ** -0.5) + rel_bias_band(rel_bias).astype(jnp.float32)[None, :, None]
    scores = jnp.where(valid[None, None, :, None, :], scores, NEG_INF)
    probs = jax.nn.softmax(scores, axis=-1).astype(v.dtype)
    out = jnp.einsum('bhnqk,bnkhd->bnqhd', probs, vb)
    return out.reshape(b, s, D_ATT)


def memory_cross_attention(h, mem_n, wq, wk, wv, wo):
    b, s, _ = h.shape
    m = mem_n.shape[1]
    q = (h @ wq).reshape(b, s, N_XHEADS, XHEAD_DIM)
    k = (mem_n @ wk).reshape(b, m, N_XHEADS, XHEAD_DIM)
    v = (mem_n @ wv).reshape(b, m, N_XHEADS, XHEAD_DIM)
    scores = jnp.einsum('bshd,bmhd->bhsm', q, k, preferred_element_type=jnp.float32) * (XHEAD_DIM ** -0.5)
    probs = jax.nn.softmax(scores, axis=-1).astype(v.dtype)
    out = jnp.einsum('bhsm,bmhd->bshd', probs, v).reshape(b, s, D_MODEL)
    return out @ wo


def routed_experts(ht, expert_idx, combine_w, w_gate, w_up, w_down):
    t, d = ht.shape
    n_slots = t * TOP_K
    flat_e = expert_idx.reshape(-1)
    flat_tok = jnp.repeat(jnp.arange(t, dtype=jnp.int32), TOP_K)
    order = jnp.argsort(flat_e)
    e_sorted = flat_e[order]
    tok_sorted = flat_tok[order]
    w_sorted = combine_w.reshape(-1)[order]
    counts = jnp.bincount(flat_e, length=N_EXPERTS)
    padded = (counts + MOE_BLOCK - 1) // MOE_BLOCK * MOE_BLOCK
    pad_end = jnp.cumsum(padded)
    pad_start = pad_end - padded
    start = jnp.cumsum(counts) - counts
    dest = pad_start[e_sorted] + (jnp.arange(n_slots) - start[e_sorted])
    n_blocks = (n_slots + MOE_BLOCK - 1) // MOE_BLOCK + N_EXPERTS
    n_rows = n_blocks * MOE_BLOCK
    src_tok = jnp.full((n_rows,), t, dtype=jnp.int32).at[dest].set(tok_sorted)
    x_pad = jnp.concatenate([ht, jnp.zeros((1, d), ht.dtype)], axis=0)
    xs = x_pad[src_tok].reshape(n_blocks, MOE_BLOCK, d)
    block_expert = jnp.minimum(
        jnp.searchsorted(pad_end, jnp.arange(n_blocks) * MOE_BLOCK, side='right'), N_EXPERTS - 1)

    def expert_block(args):
        xb, e = args
        g = xb @ w_gate[e]
        u = xb @ w_up[e]
        return (jax.nn.silu(g) * u) @ w_down[e]

    ys = lax.map(expert_block, (xs, block_expert)).reshape(n_rows, d)
    contrib = (ys[dest].astype(jnp.float32) * w_sorted[:, None]).astype(ht.dtype)
    return jnp.zeros((t, d), ht.dtype).at[tok_sorted].add(contrib)


def hierarchical_moe(h, w_group, b_group, w_router, b_router, w_gate, w_up, w_down):
    b, s, d = h.shape
    t = b * s
    ht = h.reshape(t, d)
    g_logits = (ht @ w_group).astype(jnp.float32) + b_group.astype(jnp.float32)
    g_probs = jax.nn.softmax(g_logits, axis=-1)
    g_p, g_idx = lax.top_k(g_probs, 1)
    e_all = jnp.einsum('td,gde->tge', ht, w_router).astype(jnp.float32) + b_router.astype(jnp.float32)
    sel = jnp.broadcast_to(g_idx[:, :, None], (t, 1, EXPERTS_PER_GROUP))
    e_logits = jnp.take_along_axis(e_all, sel, axis=1)[:, 0]
    e_probs = jax.nn.softmax(e_logits, axis=-1)
    e_p, e_idx = lax.top_k(e_probs, TOP_K)
    e_p = e_p / jnp.sum(e_p, axis=-1, keepdims=True)
    combine_w = g_p * e_p
    expert_idx = g_idx * EXPERTS_PER_GROUP + e_idx
    y = routed_experts(ht, expert_idx, combine_w, w_gate, w_up, w_down)
    return y.reshape(b, s, d)


def setup_inputs(seed: int = 0) -> dict:
    key = jax.random.key(seed)
    ks = jax.random.split(key, 32)
    f32 = jnp.float32

    def nrm(k, shape, scale):
        return jax.random.normal(k, shape, f32) * scale

    def gain(k, shape):
        return 1.0 + 0.02 * jax.random.normal(k, shape, f32)

    a0 = jax.random.uniform(ks[10], (DEPTH, D_RNN), f32, 0.9, 0.999)
    lam = jnp.log(a0) - jnp.log1p(-a0)
    return {
        'x': nrm(ks[0], (BATCH, SEQ, D_MODEL), 1.0),
        'mem': nrm(ks[1], (BATCH, N_MEM, D_MODEL), 1.0),
        'ln1_w': gain(ks[2], (DEPTH, D_MODEL)),
        'w_in': nrm(ks[3], (DEPTH, D_MODEL, D_IN_PROJ), D_MODEL ** -0.5),
        'conv_w': nrm(ks[4], (DEPTH, CONV_WIDTH, D_RNN), CONV_WIDTH ** -0.5),
        'conv_b': nrm(ks[5], (DEPTH, D_RNN), 0.02),
        'rnn_wa': nrm(ks[6], (DEPTH, N_RNN_BLOCKS, RNN_BLOCK, RNN_BLOCK), RNN_BLOCK ** -0.5),
        'rnn_ba': nrm(ks[7], (DEPTH, D_RNN), 0.02),
        'rnn_wx': nrm(ks[8], (DEPTH, N_RNN_BLOCKS, RNN_BLOCK, RNN_BLOCK), RNN_BLOCK ** -0.5),
        'rnn_bx': nrm(ks[9], (DEPTH, D_RNN), 0.02),
        'rnn_lambda': lam,
        'rel_bias': nrm(ks[11], (DEPTH, N_ATT_HEADS, 2 * REL_CLIP + 1), 0.2),
        'gn_rnn_w': gain(ks[12], (DEPTH, D_RNN)),
        'gn_att_w': gain(ks[13], (DEPTH, D_ATT)),
        'w_out': nrm(ks[14], (DEPTH, D_MIX, D_MODEL), D_MIX ** -0.5),
        'ln2_w': gain(ks[15], (DEPTH, D_MODEL)),
        'mem_norm_w': gain(ks[16], (D_MODEL,)),
        'xq_w': nrm(ks[17], (DEPTH, D_MODEL, D_MODEL), D_MODEL ** -0.5),
        'xk_w': nrm(ks[18], (DEPTH, D_MODEL, D_MODEL), D_MODEL ** -0.5),
        'xv_w': nrm(ks[19], (DEPTH, D_MODEL, D_MODEL), D_MODEL ** -0.5),
        'xo_w': nrm(ks[20], (DEPTH, D_MODEL, D_MODEL), D_MODEL ** -0.5),
        'ln3_w': gain(ks[21], (DEPTH, D_MODEL)),
        'router_group_w': nrm(ks[22], (DEPTH, D_MODEL, N_GROUPS), D_MODEL ** -0.5),
        'router_group_b': nrm(ks[23], (DEPTH, N_GROUPS), 0.01),
        'router_expert_w': nrm(ks[24], (DEPTH, N_GROUPS, D_MODEL, EXPERTS_PER_GROUP), D_MODEL ** -0.5),
        'router_expert_b': nrm(ks[25], (DEPTH, N_GROUPS, EXPERTS_PER_GROUP), 0.01),
        'expert_gate_w': nrm(ks[26], (DEPTH, N_EXPERTS, D_MODEL, D_EXPERT), D_MODEL ** -0.5),
        'expert_up_w': nrm(ks[27], (DEPTH, N_EXPERTS, D_MODEL, D_EXPERT), D_MODEL ** -0.5),
        'expert_down_w': nrm(ks[28], (DEPTH, N_EXPERTS, D_EXPERT, D_MODEL), D_EXPERT ** -0.5),
        'final_norm_w': gain(ks[29], (D_MODEL,)),
    }


def reference(x, mem, ln1_w, w_in, conv_w, conv_b, rnn_wa, rnn_ba, rnn_wx, rnn_bx, rnn_lambda,
              rel_bias, gn_rnn_w, gn_att_w, w_out, ln2_w, mem_norm_w, xq_w, xk_w, xv_w, xo_w,
              ln3_w, router_group_w, router_group_b, router_expert_w, router_expert_b,
              expert_gate_w, expert_up_w, expert_down_w, final_norm_w):
    mem_n = rms_norm(mem, mem_norm_w)
    for l in range(DEPTH):
        h = rms_norm(x, ln1_w[l])
        proj = h @ w_in[l]
        xr = proj[..., :D_RNN]
        xg = proj[..., D_RNN:2 * D_RNN]
        q = proj[..., 2 * D_RNN:2 * D_RNN + D_ATT]
        k = proj[..., 2 * D_RNN + D_ATT:2 * D_RNN + 2 * D_ATT]
        v = proj[..., 2 * D_RNN + 2 * D_ATT:]
        y_rnn = recurrent_group(xr, xg, conv_w[l], conv_b[l], rnn_wa[l], rnn_ba[l],
                                rnn_wx[l], rnn_bx[l], rnn_lambda[l])
        y_att = chunk_band_attention(q, k, v, rel_bias[l])
        mixed = jnp.concatenate([rms_norm(y_rnn, gn_rnn_w[l]), rms_norm(y_att, gn_att_w[l])], axis=-1)
        x = x + mixed @ w_out[l]
        h = rms_norm(x, ln2_w[l])
        x = x + memory_cross_attention(h, mem_n, xq_w[l], xk_w[l], xv_w[l], xo_w[l])
        h = rms_norm(x, ln3_w[l])
        x = x + hierarchical_moe(h, router_group_w[l], router_group_b[l], router_expert_w[l],
                                 router_expert_b[l], expert_gate_w[l], expert_up_w[l], expert_down_w[l])
    return rms_norm(x, final_norm_w)
```

```python
import functools

import jax
import jax.numpy as jnp
import numpy as np
from jax import lax
from jax.experimental import pallas as pl
from jax.experimental.pallas import tpu as pltpu

F32 = jnp.float32
BF16 = jnp.bfloat16

D_MODEL = 1024
D_RNN = 512
D_ATT = 512
N_RNN_BLOCKS = 8
CONV_WIDTH = 4
RG_LRU_C = 8.0
N_ATT_HEADS = 8
ATT_HEAD_DIM = 64
CHUNK = 64
LEFT_CHUNKS = 8
BAND = (LEFT_CHUNKS + 1) * CHUNK
REL_CLIP = 128
N_XHEADS = 4
XHEAD_DIM = 256
N_GROUPS = 4
EXPERTS_PER_GROUP = 8
N_EXPERTS = 32
D_EXPERT = 512
EPS = 1e-6
NEG_INF = -1e30

SUBLANES = 8
SEQ_TILE = LEFT_CHUNKS * CHUNK
TOK_TILE = 512
ROW_BLOCK = 256
MOVE_TILE = 256
ROUTER_ROWS = 40
VMEM_LIMIT = 56 * 1024 * 1024


def _rms(x, w):
    return x * lax.rsqrt(jnp.mean(x * x, axis=-1, keepdims=True) + EPS) * w


def _const_spec(shape):
    return pl.BlockSpec(shape, lambda *_: (0,) * len(shape))


def _memkv_kernel(mem_ref, w_ref, wk_ref, wv_ref, k_ref, v_ref):
    mn = _rms(mem_ref[...], w_ref[...]).astype(BF16)
    k_ref[...] = jnp.dot(mn, wk_ref[...], preferred_element_type=F32).astype(BF16)
    v_ref[...] = jnp.dot(mn, wv_ref[...], preferred_element_type=F32).astype(BF16)


def _memkv_call(mem, mem_norm_w, wk, wv):
    b, m, d = mem.shape
    return pl.pallas_call(
        _memkv_kernel,
        grid=(b,),
        in_specs=[
            pl.BlockSpec((None, m, d), lambda i: (i, 0, 0)),
            _const_spec((1, d)),
            _const_spec((d, d)),
            _const_spec((d, d)),
        ],
        out_specs=[
            pl.BlockSpec((None, m, d), lambda i: (i, 0, 0)),
            pl.BlockSpec((None, m, d), lambda i: (i, 0, 0)),
        ],
        out_shape=[jax.ShapeDtypeStruct((b, m, d), BF16)] * 2,
        compiler_params=pltpu.CompilerParams(vmem_limit_bytes=VMEM_LIMIT),
        name="mem_kv",
    )(mem, mem_norm_w, wk, wv)


def _mixer_kernel(x_ref, ln1_ref, win_ref, convw_ref, convb_ref, wa_ref, ba_ref, wx_ref, bx_ref,
                  lam_ref, bias_ref, gnr_ref, gna_ref, wout_ref, o_ref,
                  xr_ext, a_s, b_s, h_state, q_s, kbuf, vbuf, yatt):
    ts = SEQ_TILE
    i = pl.program_id(1)

    @pl.when(i == 0)
    def _():
        xr_ext[pl.ds(0, SUBLANES), :] = jnp.zeros((SUBLANES, D_RNN), F32)
        h_state[...] = jnp.zeros_like(h_state)
        kbuf[pl.ds(0, ts), :] = jnp.zeros((ts, D_ATT), BF16)
        vbuf[pl.ds(0, ts), :] = jnp.zeros((ts, D_ATT), BF16)

    x = x_ref[...]
    h = _rms(x, ln1_ref[...]).astype(BF16)

    def proj(lo):
        return jnp.dot(h, win_ref[:, lo:lo + D_RNN], preferred_element_type=F32)

    xr_ext[pl.ds(SUBLANES, ts), :] = proj(0)
    xc = convb_ref[...] + sum(
        convw_ref[j:j + 1, :] * xr_ext[pl.ds(SUBLANES - (CONV_WIDTH - 1) + j, ts), :]
        for j in range(CONV_WIDTH))
    xr_ext[pl.ds(0, SUBLANES), :] = xr_ext[pl.ds(ts, SUBLANES), :]
    xcb = xc.astype(BF16)
    r = jax.nn.sigmoid(jnp.dot(xcb, wa_ref[...], preferred_element_type=F32) + ba_ref[...])
    gate_i = jax.nn.sigmoid(jnp.dot(xcb, wx_ref[...], preferred_element_type=F32) + bx_ref[...])
    z = -lam_ref[...]
    softplus = jnp.maximum(z, 0.0) + jnp.log1p(jnp.exp(-jnp.abs(z)))
    log_a = (-RG_LRU_C * r) * softplus
    a = jnp.exp(log_a)
    t = jnp.tanh(log_a)
    u = jnp.sqrt(-2.0 * t / (1.0 - t)) * (gate_i * xc)

    ng = ts // SUBLANES
    a3 = a.reshape(ng, SUBLANES, D_RNN)
    b3 = u.reshape(ng, SUBLANES, D_RNN)
    row = lax.broadcasted_iota(jnp.int32, (ng, SUBLANES, D_RNN), 1)
    for d in (1, 2, 4):
        a_sh = pltpu.roll(a3, d, 1)
        b_sh = pltpu.roll(b3, d, 1)
        m = row >= d
        b3 = jnp.where(m, a3 * b_sh + b3, b3)
        a3 = jnp.where(m, a3 * a_sh, a3)
    a_s[...] = a3
    b_s[...] = b3

    def carry(g, hprev):
        hg = a_s[g] * hprev + b_s[g]
        b_s[g] = hg
        return jnp.broadcast_to(hg[SUBLANES - 1:SUBLANES, :], (SUBLANES, D_RNN))

    h_state[...] = lax.fori_loop(0, ng, carry, h_state[...])
    hseq = b_s[...].reshape(ts, D_RNN)
    xg = proj(D_RNN)
    gelu = 0.5 * xg * (1.0 + jnp.tanh(0.7978845608028654 * (xg + 0.044715 * (xg * xg * xg))))
    y_rnn = _rms(gelu * hseq, gnr_ref[...]).astype(BF16)

    q_s[...] = (proj(2 * D_RNN) * (ATT_HEAD_DIM ** -0.5)).astype(BF16)
    kbuf[pl.ds(ts, ts), :] = proj(2 * D_RNN + D_ATT).astype(BF16)
    vbuf[pl.ds(ts, ts), :] = proj(2 * D_RNN + 2 * D_ATT).astype(BF16)
    key_pos = lax.broadcasted_iota(jnp.int32, (CHUNK, BAND), 1)

    def chunk_body(j, carry_):
        base = pl.multiple_of(j * CHUNK, CHUNK)
        qj = q_s[pl.ds(base, CHUNK), :]
        kb = kbuf[pl.ds(base, BAND), :]
        vb = vbuf[pl.ds(base, BAND), :]
        valid = key_pos >= (LEFT_CHUNKS - (i * LEFT_CHUNKS + j)) * CHUNK
        outs = []
        for hd in range(N_ATT_HEADS):
            sl = slice(hd * ATT_HEAD_DIM, (hd + 1) * ATT_HEAD_DIM)
            s = lax.dot_general(qj[:, sl], kb[:, sl], (((1,), (1,)), ((), ())),
                                preferred_element_type=F32)
            s = jnp.where(valid, s + bias_ref[hd], NEG_INF)
            p = jnp.exp(s - jnp.max(s, axis=-1, keepdims=True))
            l = jnp.sum(p, axis=-1, keepdims=True)
            o = jnp.dot(p.astype(BF16), vb[:, sl], preferred_element_type=F32)
            outs.append(o / l)
        yatt[pl.ds(base, CHUNK), :] = jnp.concatenate(outs, axis=-1)
        return carry_

    lax.fori_loop(0, ts // CHUNK, chunk_body, 0)
    kbuf[pl.ds(0, ts), :] = kbuf[pl.ds(ts, ts), :]
    vbuf[pl.ds(0, ts), :] = vbuf[pl.ds(ts, ts), :]
    y_att = _rms(yatt[...], gna_ref[...]).astype(BF16)

    mix = (jnp.dot(y_rnn, wout_ref[pl.ds(0, D_RNN), :], preferred_element_type=F32)
           + jnp.dot(y_att, wout_ref[pl.ds(D_RNN, D_ATT), :], preferred_element_type=F32))
    o_ref[...] = x + mix


def _mixer_call(x, ln1_w, w_in, conv_w, conv_b, wa_bd, ba, wx_bd, bx, lam, bias_band, gnr, gna, w_out):
    b, s, d = x.shape
    ts = SEQ_TILE
    assert s % ts == 0
    dp = w_in.shape[1]
    return pl.pallas_call(
        _mixer_kernel,
        grid=(b, s // ts),
        in_specs=[
            pl.BlockSpec((None, ts, d), lambda bi, i: (bi, i, 0)),
            _const_spec((1, d)),
            _const_spec((d, dp)),
            _const_spec((CONV_WIDTH, D_RNN)),
            _const_spec((1, D_RNN)),
            _const_spec((D_RNN, D_RNN)),
            _const_spec((1, D_RNN)),
            _const_spec((D_RNN, D_RNN)),
            _const_spec((1, D_RNN)),
            _const_spec((1, D_RNN)),
            _const_spec((N_ATT_HEADS, CHUNK, BAND)),
            _const_spec((1, D_RNN)),
            _const_spec((1, D_ATT)),
            _const_spec((d, d)),
        ],
        out_specs=pl.BlockSpec((None, ts, d), lambda bi, i: (bi, i, 0)),
        out_shape=jax.ShapeDtypeStruct((b, s, d), F32),
        scratch_shapes=[
            pltpu.VMEM((ts + SUBLANES, D_RNN), F32),
            pltpu.VMEM((ts // SUBLANES, SUBLANES, D_RNN), F32),
            pltpu.VMEM((ts // SUBLANES, SUBLANES, D_RNN), F32),
            pltpu.VMEM((SUBLANES, D_RNN), F32),
            pltpu.VMEM((ts, D_ATT), BF16),
            pltpu.VMEM((2 * ts, D_ATT), BF16),
            pltpu.VMEM((2 * ts, D_ATT), BF16),
            pltpu.VMEM((ts, D_ATT), F32),
        ],
        compiler_params=pltpu.CompilerParams(
            dimension_semantics=("arbitrary", "arbitrary"), vmem_limit_bytes=VMEM_LIMIT),
        name="mixer",
    )(x, ln1_w, w_in, conv_w, conv_b, wa_bd, ba, wx_bd, bx, lam, bias_band, gnr, gna, w_out)


def _xattn_router_kernel(x_ref, ln2_ref, wq_ref, k_ref, v_ref, wo_ref, ln3_ref, wr_ref, br_ref,
                         x2_ref, h3_ref, eidx_ref, cw_ref):
    x = x_ref[...]
    h = _rms(x, ln2_ref[...]).astype(BF16)
    q = (jnp.dot(h, wq_ref[...], preferred_element_type=F32) * (XHEAD_DIM ** -0.5)).astype(BF16)
    outs = []
    for hd in range(N_XHEADS):
        sl = slice(hd * XHEAD_DIM, (hd + 1) * XHEAD_DIM)
        s = lax.dot_general(q[:, sl], k_ref[:, sl], (((1,), (1,)), ((), ())),
                            preferred_element_type=F32)
        p = jnp.exp(s - jnp.max(s, axis=-1, keepdims=True))
        l = jnp.sum(p, axis=-1, keepdims=True)
        o = jnp.dot(p.astype(BF16), v_ref[:, sl], preferred_element_type=F32)
        outs.append((o / l).astype(BF16))
    att = jnp.concatenate(outs, axis=-1)
    x2 = x + jnp.dot(att, wo_ref[...], preferred_element_type=F32)
    x2_ref[...] = x2

    h3 = _rms(x2, ln3_ref[...])
    h3_ref[...] = h3
    logits = lax.dot_general(wr_ref[...], h3.astype(BF16), (((1,), (1,)), ((), ())),
                             preferred_element_type=F32) + br_ref[...]
    gl = logits[0:N_GROUPS, :]
    ge = jnp.exp(gl - jnp.max(gl, axis=0, keepdims=True))
    gp = ge / jnp.sum(ge, axis=0, keepdims=True)
    g_p = jnp.max(gp, axis=0, keepdims=True)
    g_iota = lax.broadcasted_iota(jnp.int32, gp.shape, 0)
    g_idx = jnp.min(jnp.where(gp == g_p, g_iota, N_GROUPS), axis=0, keepdims=True)

    el = jnp.zeros((EXPERTS_PER_GROUP, logits.shape[1]), F32)
    for g in range(N_GROUPS):
        lo = SUBLANES + g * EXPERTS_PER_GROUP
        el = jnp.where(g_idx == g, logits[lo:lo + EXPERTS_PER_GROUP, :], el)
    ee = jnp.exp(el - jnp.max(el, axis=0, keepdims=True))
    ep = ee / jnp.sum(ee, axis=0, keepdims=True)
    e_iota = lax.broadcasted_iota(jnp.int32, ep.shape, 0)
    p1 = jnp.max(ep, axis=0, keepdims=True)
    i1 = jnp.min(jnp.where(ep == p1, e_iota, EXPERTS_PER_GROUP), axis=0, keepdims=True)
    ep2 = jnp.where(e_iota == i1, -1.0, ep)
    p2 = jnp.max(ep2, axis=0, keepdims=True)
    i2 = jnp.min(jnp.where(ep2 == p2, e_iota, EXPERTS_PER_GROUP), axis=0, keepdims=True)
    den = p1 + p2
    eidx_ref[...] = jnp.concatenate(
        [g_idx * EXPERTS_PER_GROUP + i1, g_idx * EXPERTS_PER_GROUP + i2], axis=0)
    cw_ref[...] = jnp.concatenate([g_p * (p1 / den), g_p * (p2 / den)], axis=0)


def _xattn_router_call(x1, ln2_w, wq, kmem, vmem, wo, ln3_w, wr_t, br_t):
    b, s, d = x1.shape
    tt = TOK_TILE
    m = kmem.shape[1]
    return pl.pallas_call(
        _xattn_router_kernel,
        grid=(b, s // tt),
        in_specs=[
            pl.BlockSpec((None, tt, d), lambda bi, i: (bi, i, 0)),
            _const_spec((1, d)),
            _const_spec((d, d)),
            pl.BlockSpec((None, m, d), lambda bi, i: (bi, 0, 0)),
            pl.BlockSpec((None, m, d), lambda bi, i: (bi, 0, 0)),
            _const_spec((d, d)),
            _const_spec((1, d)),
            _const_spec((ROUTER_ROWS, d)),
            _const_spec((ROUTER_ROWS, 1)),
        ],
        out_specs=[
            pl.BlockSpec((None, tt, d), lambda bi, i: (bi, i, 0)),
            pl.BlockSpec((None, tt, d), lambda bi, i: (bi, i, 0)),
            pl.BlockSpec((None, 2, tt), lambda bi, i: (bi, 0, i)),
            pl.BlockSpec((None, 2, tt), lambda bi, i: (bi, 0, i)),
        ],
        out_shape=[
            jax.ShapeDtypeStruct((b, s, d), F32),
            jax.ShapeDtypeStruct((b, s, d), F32),
            jax.ShapeDtypeStruct((b, 2, s), jnp.int32),
            jax.ShapeDtypeStruct((b, 2, s), F32),
        ],
        compiler_params=pltpu.CompilerParams(
            dimension_semantics=("arbitrary", "arbitrary"), vmem_limit_bytes=VMEM_LIMIT),
        name="xattn_router",
    )(x1, ln2_w, wq, kmem, vmem, wo, ln3_w, wr_t, br_t)


def _dispatch_kernel(tail_ref, na_ref, d0_ref, d1_ref, h_ref, xs_ref, zbuf, sem, zsem):
    i = pl.program_id(0)
    tm = MOVE_TILE
    nb = xs_ref.shape[0] // ROW_BLOCK

    @pl.when(i == 0)
    def _():
        zbuf[...] = jnp.zeros_like(zbuf)

        def zero_copy(start_row):
            start_row = pl.multiple_of(start_row, ROW_BLOCK)
            return pltpu.make_async_copy(zbuf, xs_ref.at[pl.ds(start_row, ROW_BLOCK), :], zsem)

        def start(e, c):
            @pl.when(tail_ref[e] >= 0)
            def _():
                zero_copy(tail_ref[e]).start()
            return c

        def wait(e, c):
            @pl.when(tail_ref[e] >= 0)
            def _():
                zero_copy(tail_ref[e]).wait()
            return c

        def start_unused(b, c):
            zero_copy(b * ROW_BLOCK).start()
            return c

        def wait_unused(b, c):
            zero_copy(b * ROW_BLOCK).wait()
            return c

        lax.fori_loop(0, N_EXPERTS, start, 0)
        lax.fori_loop(na_ref[0], nb, start_unused, 0)
        lax.fori_loop(0, N_EXPERTS, wait, 0)
        lax.fori_loop(na_ref[0], nb, wait_unused, 0)

    def row_copy(t, dst_row):
        return pltpu.make_async_copy(h_ref.at[pl.ds(t, 1), :], xs_ref.at[pl.ds(dst_row, 1), :], sem)

    def start(t, c):
        row_copy(t, d0_ref[0, t]).start()
        row_copy(t, d1_ref[0, t]).start()
        return c

    def wait(t, c):
        row_copy(t, 0).wait()
        row_copy(t, 0).wait()
        return c

    lax.fori_loop(0, tm, start, 0, unroll=8)
    lax.fori_loop(0, tm, wait, 0, unroll=8)


def _dispatch_call(tail_start, n_active, dest0, dest1, h3, n_rows):
    t, d = h3.shape
    tm = MOVE_TILE
    nt = t // tm
    grid_spec = pltpu.PrefetchScalarGridSpec(
        num_scalar_prefetch=2,
        grid=(nt,),
        in_specs=[
            pl.BlockSpec((None, 1, tm), lambda i, *_: (i, 0, 0), memory_space=pltpu.SMEM),
            pl.BlockSpec((None, 1, tm), lambda i, *_: (i, 0, 0), memory_space=pltpu.SMEM),
            pl.BlockSpec((tm, d), lambda i, *_: (i, 0)),
        ],
        out_specs=pl.BlockSpec(memory_space=pl.ANY),
        scratch_shapes=[
            pltpu.VMEM((ROW_BLOCK, d), F32),
            pltpu.SemaphoreType.DMA(()),
            pltpu.SemaphoreType.DMA(()),
        ],
    )
    return pl.pallas_call(
        _dispatch_kernel,
        grid_spec=grid_spec,
        out_shape=jax.ShapeDtypeStruct((n_rows, d), F32),
        compiler_params=pltpu.CompilerParams(
            dimension_semantics=("arbitrary",), vmem_limit_bytes=VMEM_LIMIT,
            has_side_effects=True),
        name="dispatch",
    )(tail_start, n_active, dest0.reshape(nt, 1, tm), dest1.reshape(nt, 1, tm), h3)


def _experts_kernel(be_ref, na_ref, xs_ref, wg_ref, wu_ref, wd_ref, ys_ref, wg_s, wu_s, wd_s, prev_e):
    b = pl.program_id(0)

    @pl.when(b == 0)
    def _():
        prev_e[0] = -1

    @pl.when(b < na_ref[0])
    def _():
        e = be_ref[b]

        @pl.when(e != prev_e[0])
        def _():
            wg_s[...] = wg_ref[...].astype(BF16)
            wu_s[...] = wu_ref[...].astype(BF16)
            wd_s[...] = wd_ref[...].astype(BF16)
            prev_e[0] = e

        xb = xs_ref[...].astype(BF16)
        g = jnp.dot(xb, wg_s[...], preferred_element_type=F32)
        u = jnp.dot(xb, wu_s[...], preferred_element_type=F32)
        hmid = (g * jax.nn.sigmoid(g) * u).astype(BF16)
        ys_ref[...] = jnp.dot(hmid, wd_s[...], preferred_element_type=F32)

    @pl.when(b >= na_ref[0])
    def _():
        ys_ref[...] = jnp.zeros_like(ys_ref)


def _experts_call(block_expert, n_active, xs, w_gate, w_up, w_down):
    n_rows, d = xs.shape
    nb = n_rows // ROW_BLOCK
    de = w_gate.shape[-1]

    def blk(b, be, na):
        return jnp.minimum(b, na[0] - 1)

    grid_spec = pltpu.PrefetchScalarGridSpec(
        num_scalar_prefetch=2,
        grid=(nb,),
        in_specs=[
            pl.BlockSpec((ROW_BLOCK, d), lambda b, be, na: (blk(b, be, na), 0)),
            pl.BlockSpec((None, d, de), lambda b, be, na: (be[blk(b, be, na)], 0, 0)),
            pl.BlockSpec((None, d, de), lambda b, be, na: (be[blk(b, be, na)], 0, 0)),
            pl.BlockSpec((None, de, d), lambda b, be, na: (be[blk(b, be, na)], 0, 0)),
        ],
        out_specs=pl.BlockSpec((ROW_BLOCK, d), lambda b, be, na: (b, 0)),
        scratch_shapes=[
            pltpu.VMEM((d, de), BF16),
            pltpu.VMEM((d, de), BF16),
            pltpu.VMEM((de, d), BF16),
            pltpu.SMEM((1,), jnp.int32),
        ],
    )
    return pl.pallas_call(
        _experts_kernel,
        grid_spec=grid_spec,
        out_shape=jax.ShapeDtypeStruct((n_rows, d), F32),
        compiler_params=pltpu.CompilerParams(
            dimension_semantics=("arbitrary",), vmem_limit_bytes=VMEM_LIMIT),
        name="experts",
    )(block_expert, n_active, xs, w_gate, w_up, w_down)


def _combine_kernel(d0_ref, d1_ref, x_ref, cw_ref, fw_ref, ys_ref, o_ref, y0, y1, sem):
    tm = MOVE_TILE

    def row_copy(src_row, buf, t):
        return pltpu.make_async_copy(ys_ref.at[pl.ds(src_row, 1), :], buf.at[pl.ds(t, 1), :], sem)

    def start(t, c):
        row_copy(d0_ref[0, t], y0, t).start()
        row_copy(d1_ref[0, t], y1, t).start()
        return c

    def wait(t, c):
        row_copy(0, y0, t).wait()
        row_copy(0, y1, t).wait()
        return c

    lax.fori_loop(0, tm, start, 0, unroll=8)
    lax.fori_loop(0, tm, wait, 0, unroll=8)
    cw = cw_ref[...]
    moe = cw[:, 0:1] * y0[...] + cw[:, 1:2] * y1[...]
    o_ref[...] = _rms(x_ref[...] + moe, fw_ref[...])


def _combine_call(dest0, dest1, x2, cw_t, final_w, ys):
    t, d = x2.shape
    tm = MOVE_TILE
    nt = t // tm
    return pl.pallas_call(
        _combine_kernel,
        grid=(nt,),
        in_specs=[
            pl.BlockSpec((None, 1, tm), lambda i: (i, 0, 0), memory_space=pltpu.SMEM),
            pl.BlockSpec((None, 1, tm), lambda i: (i, 0, 0), memory_space=pltpu.SMEM),
            pl.BlockSpec((tm, d), lambda i: (i, 0)),
            pl.BlockSpec((tm, 2), lambda i: (i, 0)),
            _const_spec((1, d)),
            pl.BlockSpec(memory_space=pl.ANY),
        ],
        out_specs=pl.BlockSpec((tm, d), lambda i: (i, 0)),
        out_shape=jax.ShapeDtypeStruct((t, d), F32),
        scratch_shapes=[
            pltpu.VMEM((tm, d), F32),
            pltpu.VMEM((tm, d), F32),
            pltpu.SemaphoreType.DMA(()),
        ],
        compiler_params=pltpu.CompilerParams(
            dimension_semantics=("arbitrary",), vmem_limit_bytes=VMEM_LIMIT),
        name="combine",
    )(dest0.reshape(nt, 1, tm), dest1.reshape(nt, 1, tm), x2, cw_t, final_w, ys)


def _block_diag(w):
    n, c, d = w.shape
    eye = jnp.eye(n, dtype=w.dtype)
    return (w[:, :, None, :] * eye[:, None, :, None]).reshape(n * c, n * d)


def _rel_bias_band(rel_bias):
    qi = np.arange(CHUNK)[:, None]
    kj = np.arange(BAND)[None, :]
    idx = np.clip(LEFT_CHUNKS * CHUNK + qi - kj, -REL_CLIP, REL_CLIP) + REL_CLIP
    return rel_bias[:, idx]


def _routing_plan(eidx, n_blocks):
    flat_e = eidx.reshape(-1)
    onehot = (flat_e[:, None] == jnp.arange(N_EXPERTS, dtype=jnp.int32)[None, :]).astype(jnp.int32)
    csum = jnp.cumsum(onehot, axis=0)
    counts = csum[-1]
    rank = jnp.sum(onehot * csum, axis=1) - 1
    padded = (counts + ROW_BLOCK - 1) // ROW_BLOCK * ROW_BLOCK
    pad_end = jnp.cumsum(padded)
    pad_start = pad_end - padded
    dest = jnp.sum(onehot * pad_start[None, :], axis=1) + rank
    block_row = jnp.arange(n_blocks, dtype=jnp.int32) * ROW_BLOCK
    block_expert = jnp.minimum(
        jnp.sum((pad_end[None, :] <= block_row[:, None]).astype(jnp.int32), axis=1), N_EXPERTS - 1)
    n_active = (pad_end[-1] // ROW_BLOCK).astype(jnp.int32).reshape(1)
    tail_start = jnp.where(padded > 0, pad_end - ROW_BLOCK, -1).astype(jnp.int32)
    return dest.astype(jnp.int32).reshape(eidx.shape), block_expert, n_active, tail_start


def kernel(x, mem, ln1_w, w_in, conv_w, conv_b, rnn_wa, rnn_ba, rnn_wx, rnn_bx, rnn_lambda, rel_bias, gn_rnn_w, gn_att_w, w_out, ln2_w, mem_norm_w, xq_w, xk_w, xv_w, xo_w, ln3_w, router_group_w, router_group_b, router_expert_w, router_expert_b, expert_gate_w, expert_up_w, expert_down_w, final_norm_w):
    b, s, d = x.shape
    t = b * s
    depth = ln1_w.shape[0]
    row = lambda v: v.reshape(1, -1)

    kmem, vmem = None, None
    for l in range(depth):
        x = _mixer_call(
            x, row(ln1_w[l]), w_in[l].astype(BF16), conv_w[l], row(conv_b[l]),
            _block_diag(rnn_wa[l]).astype(BF16), row(rnn_ba[l]),
            _block_diag(rnn_wx[l]).astype(BF16), row(rnn_bx[l]), row(rnn_lambda[l]),
            _rel_bias_band(rel_bias[l]), row(gn_rnn_w[l]), row(gn_att_w[l]), w_out[l].astype(BF16))

        kmem, vmem = _memkv_call(mem, row(mem_norm_w), xk_w[l].astype(BF16), xv_w[l].astype(BF16))
        pad = jnp.zeros((SUBLANES - N_GROUPS, d), F32)
        wr_t = jnp.concatenate(
            [router_group_w[l].T, pad,
             router_expert_w[l].transpose(0, 2, 1).reshape(N_EXPERTS, d)], axis=0).astype(BF16)
        br_t = jnp.concatenate(
            [router_group_b[l], jnp.zeros((SUBLANES - N_GROUPS,), F32),
             router_expert_b[l].reshape(-1)]).reshape(ROUTER_ROWS, 1)
        x2, h3, eidx, cw = _xattn_router_call(
            x, row(ln2_w[l]), xq_w[l].astype(BF16), kmem, vmem, xo_w[l].astype(BF16),
            row(ln3_w[l]), wr_t, br_t)

        n_blocks = (2 * t) // ROW_BLOCK + N_EXPERTS
        eidx_t = eidx.transpose(1, 0, 2).reshape(2, t)
        dest, block_expert, n_active, tail_start = _routing_plan(eidx_t, n_blocks)
        xs = _dispatch_call(tail_start, n_active, dest[0], dest[1], h3.reshape(t, d),
                            n_blocks * ROW_BLOCK)
        ys = _experts_call(block_expert, n_active, xs, expert_gate_w[l], expert_up_w[l],
                           expert_down_w[l])
        cw_t = cw.transpose(0, 2, 1).reshape(t, 2)
        is_last = l == depth - 1
        assert is_last, "only a single layer is supported"
        x = _combine_call(dest[0], dest[1], x2.reshape(t, d), cw_t, row(final_norm_w), ys)
        x = x.reshape(b, s, d)
    return x
```

```python
import functools

import jax
import jax.numpy as jnp
import numpy as np
from jax import lax
from jax.experimental import pallas as pl
from jax.experimental.pallas import tpu as pltpu

F32 = jnp.float32
BF16 = jnp.bfloat16

D_MODEL = 1024
D_RNN = 512
D_ATT = 512
N_RNN_BLOCKS = 8
CONV_WIDTH = 4
RG_LRU_C = 8.0
N_ATT_HEADS = 8
ATT_HEAD_DIM = 64
CHUNK = 64
LEFT_CHUNKS = 8
BAND = (LEFT_CHUNKS + 1) * CHUNK
PAIR_ROWS = 2 * CHUNK
PAIR_KEYS = BAND + CHUNK
REL_CLIP = 128
N_XHEADS = 4
XHEAD_DIM = 256
N_GROUPS = 4
EXPERTS_PER_GROUP = 8
N_EXPERTS = 32
D_EXPERT = 512
EPS = 1e-6
NEG_INF = -1e30

SUBLANES = 8
SEQ_TILE = LEFT_CHUNKS * CHUNK
TOK_TILE = 512
ROW_BLOCK = 256
MOVE_TILE = 256
ROUTER_ROWS = 40
VMEM_LIMIT = 56 * 1024 * 1024


def _rms(x, w):
    return x * lax.rsqrt(jnp.mean(x * x, axis=-1, keepdims=True) + EPS) * w


def _const_spec(shape):
    return pl.BlockSpec(shape, lambda *_: (0,) * len(shape))


def _memkv_kernel(mem_ref, w_ref, wk_ref, wv_ref, k_ref, v_ref):
    mn = _rms(mem_ref[...], w_ref[...]).astype(BF16)
    k_ref[...] = jnp.dot(mn, wk_ref[...], preferred_element_type=F32).astype(BF16)
    v_ref[...] = jnp.dot(mn, wv_ref[...], preferred_element_type=F32).astype(BF16)


def _memkv_call(mem, mem_norm_w, wk, wv):
    b, m, d = mem.shape
    return pl.pallas_call(
        _memkv_kernel,
        grid=(b,),
        in_specs=[
            pl.BlockSpec((None, m, d), lambda i: (i, 0, 0)),
            _const_spec((1, d)),
            _const_spec((d, d)),
            _const_spec((d, d)),
        ],
        out_specs=[
            pl.BlockSpec((None, m, d), lambda i: (i, 0, 0)),
            pl.BlockSpec((None, m, d), lambda i: (i, 0, 0)),
        ],
        out_shape=[jax.ShapeDtypeStruct((b, m, d), BF16)] * 2,
        compiler_params=pltpu.CompilerParams(vmem_limit_bytes=VMEM_LIMIT),
        name="mem_kv",
    )(mem, mem_norm_w, wk, wv)


def _mixer_kernel(x_ref, ln1_ref, win_ref, wkt_ref, convw_ref, convb_ref, wa_ref, ba_ref, wx_ref,
                  bx_ref, lam_ref, bias_ref, gnr_ref, gna_ref, wout_ref, o_ref,
                  xr_ext, a_s, b_s, h_state, q_s, kt_s, vbuf, yatt):
    ts = SEQ_TILE
    i = pl.program_id(1)

    @pl.when(i == 0)
    def _():
        xr_ext[pl.ds(0, SUBLANES), :] = jnp.zeros((SUBLANES, D_RNN), F32)
        h_state[...] = jnp.zeros_like(h_state)
        kt_s[:, pl.ds(0, ts)] = jnp.zeros((D_ATT, ts), BF16)
        vbuf[pl.ds(0, ts), :] = jnp.zeros((ts, D_ATT), BF16)

    x = x_ref[...]
    h = _rms(x, ln1_ref[...]).astype(BF16)

    def proj(lo):
        return jnp.dot(h, win_ref[:, lo:lo + D_RNN], preferred_element_type=F32)

    xr_ext[pl.ds(SUBLANES, ts), :] = proj(0)
    xc = convb_ref[...] + sum(
        convw_ref[j:j + 1, :] * xr_ext[pl.ds(SUBLANES - (CONV_WIDTH - 1) + j, ts), :]
        for j in range(CONV_WIDTH))
    xr_ext[pl.ds(0, SUBLANES), :] = xr_ext[pl.ds(ts, SUBLANES), :]
    xcb = xc.astype(BF16)
    r = jax.nn.sigmoid(jnp.dot(xcb, wa_ref[...], preferred_element_type=F32) + ba_ref[...])
    gate_i = jax.nn.sigmoid(jnp.dot(xcb, wx_ref[...], preferred_element_type=F32) + bx_ref[...])
    z = -lam_ref[...]
    softplus = jnp.maximum(z, 0.0) + jnp.log1p(jnp.exp(-jnp.abs(z)))
    log_a = (-RG_LRU_C * r) * softplus
    a = jnp.exp(log_a)
    t = jnp.tanh(log_a)
    u = jnp.sqrt(-2.0 * t / (1.0 - t)) * (gate_i * xc)

    ng = ts // SUBLANES
    a3 = a.reshape(ng, SUBLANES, D_RNN)
    b3 = u.reshape(ng, SUBLANES, D_RNN)
    row = lax.broadcasted_iota(jnp.int32, (ng, SUBLANES, D_RNN), 1)
    for d in (1, 2, 4):
        a_sh = pltpu.roll(a3, d, 1)
        b_sh = pltpu.roll(b3, d, 1)
        m = row >= d
        b3 = jnp.where(m, a3 * b_sh + b3, b3)
        a3 = jnp.where(m, a3 * a_sh, a3)
    a_s[...] = a3
    b_s[...] = b3

    def carry(g, hprev):
        hg = a_s[g] * hprev + b_s[g]
        b_s[g] = hg
        return jnp.broadcast_to(hg[SUBLANES - 1:SUBLANES, :], (SUBLANES, D_RNN))

    h_state[...] = lax.fori_loop(0, ng, carry, h_state[...])
    hseq = b_s[...].reshape(ts, D_RNN)
    xg = proj(D_RNN)
    gelu = 0.5 * xg * (1.0 + jnp.tanh(0.7978845608028654 * (xg + 0.044715 * (xg * xg * xg))))
    y_rnn = _rms(gelu * hseq, gnr_ref[...]).astype(BF16)

    q_s[...] = (proj(2 * D_RNN) * (ATT_HEAD_DIM ** -0.5)).astype(BF16)
    kt_s[:, pl.ds(ts, ts)] = lax.dot_general(
        wkt_ref[...], h, (((1,), (1,)), ((), ())), preferred_element_type=F32).astype(BF16)
    vbuf[pl.ds(ts, ts), :] = proj(2 * D_RNN + 2 * D_ATT).astype(BF16)
    lane = lax.broadcasted_iota(jnp.int32, (PAIR_ROWS, PAIR_ROWS), 1)
    first_head = lane < ATT_HEAD_DIM
    key_pos = lax.broadcasted_iota(jnp.int32, (PAIR_ROWS, PAIR_KEYS), 1)
    zero_q = jnp.zeros((PAIR_ROWS, PAIR_ROWS), BF16)

    for jj in range(ts // PAIR_ROWS):
        w0 = jj * PAIR_ROWS
        valid = key_pos >= jnp.where(i > 0, 0, ts - w0)
        scores = []
        for hd in range(N_ATT_HEADS):
            fl = slice((hd // 2) * PAIR_ROWS, (hd // 2 + 1) * PAIR_ROWS)
            q2 = q_s[pl.ds(w0, PAIR_ROWS), fl]
            qm = jnp.where(first_head if hd % 2 == 0 else ~first_head, q2, zero_q)
            scores.append(jnp.dot(qm, kt_s[fl, pl.ds(w0, PAIR_KEYS)], preferred_element_type=F32))
        outs = []
        for hp in range(N_ATT_HEADS // 2):
            fl = slice(hp * PAIR_ROWS, (hp + 1) * PAIR_ROWS)
            v2 = vbuf[pl.ds(w0, PAIR_KEYS), fl]
            o2 = []
            for hd in (2 * hp, 2 * hp + 1):
                s = jnp.where(valid, scores[hd] + bias_ref[hd], NEG_INF)
                p = jnp.exp(s - jnp.max(s, axis=-1, keepdims=True))
                l = jnp.sum(p, axis=-1, keepdims=True)
                o2.append(jnp.dot(p.astype(BF16), v2, preferred_element_type=F32) / l)
            outs.append(jnp.where(first_head, o2[0], o2[1]))
        yatt[pl.ds(w0, PAIR_ROWS), :] = jnp.concatenate(outs, axis=-1)

    kt_s[:, pl.ds(0, ts)] = kt_s[:, pl.ds(ts, ts)]
    vbuf[pl.ds(0, ts), :] = vbuf[pl.ds(ts, ts), :]
    y_att = _rms(yatt[...], gna_ref[...]).astype(BF16)

    mix = (jnp.dot(y_rnn, wout_ref[pl.ds(0, D_RNN), :], preferred_element_type=F32)
           + jnp.dot(y_att, wout_ref[pl.ds(D_RNN, D_ATT), :], preferred_element_type=F32))
    o_ref[...] = x + mix


def _mixer_call(x, ln1_w, w_in, wk_t, conv_w, conv_b, wa_bd, ba, wx_bd, bx, lam, bias_band, gnr, gna,
                w_out):
    b, s, d = x.shape
    ts = SEQ_TILE
    assert s % ts == 0
    dp = w_in.shape[1]
    return pl.pallas_call(
        _mixer_kernel,
        grid=(b, s // ts),
        in_specs=[
            pl.BlockSpec((None, ts, d), lambda bi, i: (bi, i, 0)),
            _const_spec((1, d)),
            _const_spec((d, dp)),
            _const_spec((D_ATT, d)),
            _const_spec((CONV_WIDTH, D_RNN)),
            _const_spec((1, D_RNN)),
            _const_spec((D_RNN, D_RNN)),
            _const_spec((1, D_RNN)),
            _const_spec((D_RNN, D_RNN)),
            _const_spec((1, D_RNN)),
            _const_spec((1, D_RNN)),
            _const_spec((N_ATT_HEADS, PAIR_ROWS, PAIR_KEYS)),
            _const_spec((1, D_RNN)),
            _const_spec((1, D_ATT)),
            _const_spec((d, d)),
        ],
        out_specs=pl.BlockSpec((None, ts, d), lambda bi, i: (bi, i, 0)),
        out_shape=jax.ShapeDtypeStruct((b, s, d), F32),
        scratch_shapes=[
            pltpu.VMEM((ts + SUBLANES, D_RNN), F32),
            pltpu.VMEM((ts // SUBLANES, SUBLANES, D_RNN), F32),
            pltpu.VMEM((ts // SUBLANES, SUBLANES, D_RNN), F32),
            pltpu.VMEM((SUBLANES, D_RNN), F32),
            pltpu.VMEM((ts, D_ATT), BF16),
            pltpu.VMEM((D_ATT, 2 * ts), BF16),
            pltpu.VMEM((2 * ts, D_ATT), BF16),
            pltpu.VMEM((ts, D_ATT), F32),
        ],
        compiler_params=pltpu.CompilerParams(
            dimension_semantics=("arbitrary", "arbitrary"), vmem_limit_bytes=VMEM_LIMIT),
        name="mixer",
    )(x, ln1_w, w_in, wk_t, conv_w, conv_b, wa_bd, ba, wx_bd, bx, lam, bias_band, gnr, gna, w_out)


def _xattn_router_kernel(x_ref, ln2_ref, wq_ref, k_ref, v_ref, wo_ref, ln3_ref, wr_ref, br_ref,
                         x2_ref, h3_ref, eidx_ref, cw_ref):
    x = x_ref[...]
    h = _rms(x, ln2_ref[...]).astype(BF16)
    q = (jnp.dot(h, wq_ref[...], preferred_element_type=F32) * (XHEAD_DIM ** -0.5)).astype(BF16)
    outs = []
    for hd in range(N_XHEADS):
        sl = slice(hd * XHEAD_DIM, (hd + 1) * XHEAD_DIM)
        s = lax.dot_general(q[:, sl], k_ref[:, sl], (((1,), (1,)), ((), ())),
                            preferred_element_type=F32)
        p = jnp.exp(s - jnp.max(s, axis=-1, keepdims=True))
        l = jnp.sum(p, axis=-1, keepdims=True)
        o = jnp.dot(p.astype(BF16), v_ref[:, sl], preferred_element_type=F32)
        outs.append((o / l).astype(BF16))
    att = jnp.concatenate(outs, axis=-1)
    x2 = x + jnp.dot(att, wo_ref[...], preferred_element_type=F32)
    x2_ref[...] = x2

    h3 = _rms(x2, ln3_ref[...])
    h3_ref[...] = h3
    logits = lax.dot_general(wr_ref[...], h3.astype(BF16), (((1,), (1,)), ((), ())),
                             preferred_element_type=F32) + br_ref[...]
    gl = logits[0:N_GROUPS, :]
    ge = jnp.exp(gl - jnp.max(gl, axis=0, keepdims=True))
    gp = ge / jnp.sum(ge, axis=0, keepdims=True)
    g_p = jnp.max(gp, axis=0, keepdims=True)
    g_iota = lax.broadcasted_iota(jnp.int32, gp.shape, 0)
    g_idx = jnp.min(jnp.where(gp == g_p, g_iota, N_GROUPS), axis=0, keepdims=True)

    el = jnp.zeros((EXPERTS_PER_GROUP, logits.shape[1]), F32)
    for g in range(N_GROUPS):
        lo = SUBLANES + g * EXPERTS_PER_GROUP
        el = jnp.where(g_idx == g, logits[lo:lo + EXPERTS_PER_GROUP, :], el)
    ee = jnp.exp(el - jnp.max(el, axis=0, keepdims=True))
    ep = ee / jnp.sum(ee, axis=0, keepdims=True)
    e_iota = lax.broadcasted_iota(jnp.int32, ep.shape, 0)
    p1 = jnp.max(ep, axis=0, keepdims=True)
    i1 = jnp.min(jnp.where(ep == p1, e_iota, EXPERTS_PER_GROUP), axis=0, keepdims=True)
    ep2 = jnp.where(e_iota == i1, -1.0, ep)
    p2 = jnp.max(ep2, axis=0, keepdims=True)
    i2 = jnp.min(jnp.where(ep2 == p2, e_iota, EXPERTS_PER_GROUP), axis=0, keepdims=True)
    den = p1 + p2
    eidx_ref[...] = jnp.concatenate(
        [g_idx * EXPERTS_PER_GROUP + i1, g_idx * EXPERTS_PER_GROUP + i2], axis=0)
    cw_ref[...] = jnp.concatenate([g_p * (p1 / den), g_p * (p2 / den)], axis=0)


def _xattn_router_call(x1, ln2_w, wq, kmem, vmem, wo, ln3_w, wr_t, br_t):
    b, s, d = x1.shape
    tt = TOK_TILE
    m = kmem.shape[1]
    return pl.pallas_call(
        _xattn_router_kernel,
        grid=(b, s // tt),
        in_specs=[
            pl.BlockSpec((None, tt, d), lambda bi, i: (bi, i, 0)),
            _const_spec((1, d)),
            _const_spec((d, d)),
            pl.BlockSpec((None, m, d), lambda bi, i: (bi, 0, 0)),
            pl.BlockSpec((None, m, d), lambda bi, i: (bi, 0, 0)),
            _const_spec((d, d)),
            _const_spec((1, d)),
            _const_spec((ROUTER_ROWS, d)),
            _const_spec((ROUTER_ROWS, 1)),
        ],
        out_specs=[
            pl.BlockSpec((None, tt, d), lambda bi, i: (bi, i, 0)),
            pl.BlockSpec((None, tt, d), lambda bi, i: (bi, i, 0)),
            pl.BlockSpec((None, 2, tt), lambda bi, i: (bi, 0, i)),
            pl.BlockSpec((None, 2, tt), lambda bi, i: (bi, 0, i)),
        ],
        out_shape=[
            jax.ShapeDtypeStruct((b, s, d), F32),
            jax.ShapeDtypeStruct((b, s, d), F32),
            jax.ShapeDtypeStruct((b, 2, s), jnp.int32),
            jax.ShapeDtypeStruct((b, 2, s), F32),
        ],
        compiler_params=pltpu.CompilerParams(
            dimension_semantics=("arbitrary", "arbitrary"), vmem_limit_bytes=VMEM_LIMIT),
        name="xattn_router",
    )(x1, ln2_w, wq, kmem, vmem, wo, ln3_w, wr_t, br_t)


def _dispatch_kernel(tail_ref, na_ref, d0_ref, d1_ref, h_ref, xs_ref, zbuf, sem, zsem):
    i = pl.program_id(0)
    tm = MOVE_TILE
    nb = xs_ref.shape[0] // ROW_BLOCK

    @pl.when(i == 0)
    def _():
        zbuf[...] = jnp.zeros_like(zbuf)

        def zero_copy(start_row):
            start_row = pl.multiple_of(start_row, ROW_BLOCK)
            return pltpu.make_async_copy(zbuf, xs_ref.at[pl.ds(start_row, ROW_BLOCK), :], zsem)

        def start(e, c):
            @pl.when(tail_ref[e] >= 0)
            def _():
                zero_copy(tail_ref[e]).start()
            return c

        def wait(e, c):
            @pl.when(tail_ref[e] >= 0)
            def _():
                zero_copy(tail_ref[e]).wait()
            return c

        def start_unused(b, c):
            zero_copy(b * ROW_BLOCK).start()
            return c

        def wait_unused(b, c):
            zero_copy(b * ROW_BLOCK).wait()
            return c

        lax.fori_loop(0, N_EXPERTS, start, 0)
        lax.fori_loop(na_ref[0], nb, start_unused, 0)
        lax.fori_loop(0, N_EXPERTS, wait, 0)
        lax.fori_loop(na_ref[0], nb, wait_unused, 0)

    def row_copy(t, dst_row):
        return pltpu.make_async_copy(h_ref.at[pl.ds(t, 1), :], xs_ref.at[pl.ds(dst_row, 1), :], sem)

    def start(t, c):
        row_copy(t, d0_ref[0, t]).start()
        row_copy(t, d1_ref[0, t]).start()
        return c

    def wait(t, c):
        row_copy(t, 0).wait()
        row_copy(t, 0).wait()
        return c

    lax.fori_loop(0, tm, start, 0, unroll=8)
    lax.fori_loop(0, tm, wait, 0, unroll=8)


def _dispatch_call(tail_start, n_active, dest0, dest1, h3, n_rows):
    t, d = h3.shape
    tm = MOVE_TILE
    nt = t // tm
    grid_spec = pltpu.PrefetchScalarGridSpec(
        num_scalar_prefetch=2,
        grid=(nt,),
        in_specs=[
            pl.BlockSpec((None, 1, tm), lambda i, *_: (i, 0, 0), memory_space=pltpu.SMEM),
            pl.BlockSpec((None, 1, tm), lambda i, *_: (i, 0, 0), memory_space=pltpu.SMEM),
            pl.BlockSpec((tm, d), lambda i, *_: (i, 0)),
        ],
        out_specs=pl.BlockSpec(memory_space=pl.ANY),
        scratch_shapes=[
            pltpu.VMEM((ROW_BLOCK, d), F32),
            pltpu.SemaphoreType.DMA(()),
            pltpu.SemaphoreType.DMA(()),
        ],
    )
    return pl.pallas_call(
        _dispatch_kernel,
        grid_spec=grid_spec,
        out_shape=jax.ShapeDtypeStruct((n_rows, d), F32),
        compiler_params=pltpu.CompilerParams(
            dimension_semantics=("arbitrary",), vmem_limit_bytes=VMEM_LIMIT,
            has_side_effects=True),
        name="dispatch",
    )(tail_start, n_active, dest0.reshape(nt, 1, tm), dest1.reshape(nt, 1, tm), h3)


def _experts_kernel(be_ref, na_ref, xs_ref, wg_ref, wu_ref, wd_ref, ys_ref, wg_s, wu_s, wd_s, prev_e):
    b = pl.program_id(0)

    @pl.when(b == 0)
    def _():
        prev_e[0] = -1

    @pl.when(b < na_ref[0])
    def _():
        e = be_ref[b]

        @pl.when(e != prev_e[0])
        def _():
            wg_s[...] = wg_ref[...].astype(BF16)
            wu_s[...] = wu_ref[...].astype(BF16)
            wd_s[...] = wd_ref[...].astype(BF16)
            prev_e[0] = e

        xb = xs_ref[...].astype(BF16)
        g = jnp.dot(xb, wg_s[...], preferred_element_type=F32)
        u = jnp.dot(xb, wu_s[...], preferred_element_type=F32)
        hmid = (g * jax.nn.sigmoid(g) * u).astype(BF16)
        ys_ref[...] = jnp.dot(hmid, wd_s[...], preferred_element_type=F32)

    @pl.when(b >= na_ref[0])
    def _():
        ys_ref[...] = jnp.zeros_like(ys_ref)


def _experts_call(block_expert, n_active, xs, w_gate, w_up, w_down):
    n_rows, d = xs.shape
    nb = n_rows // ROW_BLOCK
    de = w_gate.shape[-1]

    def blk(b, be, na):
        return jnp.minimum(b, na[0] - 1)

    grid_spec = pltpu.PrefetchScalarGridSpec(
        num_scalar_prefetch=2,
        grid=(nb,),
        in_specs=[
            pl.BlockSpec((ROW_BLOCK, d), lambda b, be, na: (blk(b, be, na), 0)),
            pl.BlockSpec((None, d, de), lambda b, be, na: (be[blk(b, be, na)], 0, 0)),
            pl.BlockSpec((None, d, de), lambda b, be, na: (be[blk(b, be, na)], 0, 0)),
            pl.BlockSpec((None, de, d), lambda b, be, na: (be[blk(b, be, na)], 0, 0)),
        ],
        out_specs=pl.BlockSpec((ROW_BLOCK, d), lambda b, be, na: (b, 0)),
        scratch_shapes=[
            pltpu.VMEM((d, de), BF16),
            pltpu.VMEM((d, de), BF16),
            pltpu.VMEM((de, d), BF16),
            pltpu.SMEM((1,), jnp.int32),
        ],
    )
    return pl.pallas_call(
        _experts_kernel,
        grid_spec=grid_spec,
        out_shape=jax.ShapeDtypeStruct((n_rows, d), F32),
        compiler_params=pltpu.CompilerParams(
            dimension_semantics=("arbitrary",), vmem_limit_bytes=VMEM_LIMIT),
        name="experts",
    )(block_expert, n_active, xs, w_gate, w_up, w_down)


def _combine_kernel(d0_ref, d1_ref, x_ref, cw_ref, fw_ref, ys_ref, o_ref, y0, y1, sem):
    tm = MOVE_TILE

    def row_copy(src_row, buf, t):
        return pltpu.make_async_copy(ys_ref.at[pl.ds(src_row, 1), :], buf.at[pl.ds(t, 1), :], sem)

    def start(t, c):
        row_copy(d0_ref[0, t], y0, t).start()
        row_copy(d1_ref[0, t], y1, t).start()
        return c

    def wait(t, c):
        row_copy(0, y0, t).wait()
        row_copy(0, y1, t).wait()
        return c

    lax.fori_loop(0, tm, start, 0, unroll=8)
    lax.fori_loop(0, tm, wait, 0, unroll=8)
    cw = cw_ref[...]
    moe = cw[:, 0:1] * y0[...] + cw[:, 1:2] * y1[...]
    o_ref[...] = _rms(x_ref[...] + moe, fw_ref[...])


def _combine_call(dest0, dest1, x2, cw_t, final_w, ys):
    t, d = x2.shape
    tm = MOVE_TILE
    nt = t // tm
    return pl.pallas_call(
        _combine_kernel,
        grid=(nt,),
        in_specs=[
            pl.BlockSpec((None, 1, tm), lambda i: (i, 0, 0), memory_space=pltpu.SMEM),
            pl.BlockSpec((None, 1, tm), lambda i: (i, 0, 0), memory_space=pltpu.SMEM),
            pl.BlockSpec((tm, d), lambda i: (i, 0)),
            pl.BlockSpec((tm, 2), lambda i: (i, 0)),
            _const_spec((1, d)),
            pl.BlockSpec(memory_space=pl.ANY),
        ],
        out_specs=pl.BlockSpec((tm, d), lambda i: (i, 0)),
        out_shape=jax.ShapeDtypeStruct((t, d), F32),
        scratch_shapes=[
            pltpu.VMEM((tm, d), F32),
            pltpu.VMEM((tm, d), F32),
            pltpu.SemaphoreType.DMA(()),
        ],
        compiler_params=pltpu.CompilerParams(
            dimension_semantics=("arbitrary",), vmem_limit_bytes=VMEM_LIMIT),
        name="combine",
    )(dest0.reshape(nt, 1, tm), dest1.reshape(nt, 1, tm), x2, cw_t, final_w, ys)


def _block_diag(w):
    n, c, d = w.shape
    eye = jnp.eye(n, dtype=w.dtype)
    return (w[:, :, None, :] * eye[:, None, :, None]).reshape(n * c, n * d)


def _rel_bias_pair(rel_bias):
    nh = rel_bias.shape[0]
    far = LEFT_CHUNKS * CHUNK - REL_CLIP
    t = jnp.concatenate(
        [rel_bias[:, REL_CLIP - (CHUNK - 1):],
         jnp.broadcast_to(rel_bias[:, -1:], (nh, far + CHUNK - 1))], axis=1)
    r = t[:, ::-1]
    band = jnp.stack([r[:, CHUNK - 1 - qi:CHUNK - 1 - qi + BAND] for qi in range(CHUNK)], axis=1)
    neg = jnp.full((nh, CHUNK, CHUNK), NEG_INF, F32)
    return jnp.concatenate(
        [jnp.concatenate([band, neg], axis=2), jnp.concatenate([neg, band], axis=2)], axis=1)


def _routing_plan(eidx, n_blocks):
    flat_e = eidx.reshape(-1)
    onehot = (flat_e[:, None] == jnp.arange(N_EXPERTS, dtype=jnp.int32)[None, :]).astype(jnp.int32)
    csum = jnp.cumsum(onehot, axis=0)
    counts = csum[-1]
    rank = jnp.sum(onehot * csum, axis=1) - 1
    padded = (counts + ROW_BLOCK - 1) // ROW_BLOCK * ROW_BLOCK
    pad_end = jnp.cumsum(padded)
    pad_start = pad_end - padded
    dest = jnp.sum(onehot * pad_start[None, :], axis=1) + rank
    block_row = jnp.arange(n_blocks, dtype=jnp.int32) * ROW_BLOCK
    block_expert = jnp.minimum(
        jnp.sum((pad_end[None, :] <= block_row[:, None]).astype(jnp.int32), axis=1), N_EXPERTS - 1)
    n_active = (pad_end[-1] // ROW_BLOCK).astype(jnp.int32).reshape(1)
    tail_start = jnp.where(padded > 0, pad_end - ROW_BLOCK, -1).astype(jnp.int32)
    return dest.astype(jnp.int32).reshape(eidx.shape), block_expert, n_active, tail_start


def kernel(x, mem, ln1_w, w_in, conv_w, conv_b, rnn_wa, rnn_ba, rnn_wx, rnn_bx, rnn_lambda, rel_bias, gn_rnn_w, gn_att_w, w_out, ln2_w, mem_norm_w, xq_w, xk_w, xv_w, xo_w, ln3_w, router_group_w, router_group_b, router_expert_w, router_expert_b, expert_gate_w, expert_up_w, expert_down_w, final_norm_w):
    b, s, d = x.shape
    t = b * s
    depth = ln1_w.shape[0]
    row = lambda v: v.reshape(1, -1)

    kmem, vmem = None, None
    for l in range(depth):
        x = _mixer_call(
            x, row(ln1_w[l]), w_in[l].astype(BF16),
            w_in[l][:, 2 * D_RNN + D_ATT:2 * D_RNN + 2 * D_ATT].T.astype(BF16),
            conv_w[l], row(conv_b[l]),
            _block_diag(rnn_wa[l]).astype(BF16), row(rnn_ba[l]),
            _block_diag(rnn_wx[l]).astype(BF16), row(rnn_bx[l]), row(rnn_lambda[l]),
            _rel_bias_pair(rel_bias[l]), row(gn_rnn_w[l]), row(gn_att_w[l]), w_out[l].astype(BF16))

        kmem, vmem = _memkv_call(mem, row(mem_norm_w), xk_w[l].astype(BF16), xv_w[l].astype(BF16))
        pad = jnp.zeros((SUBLANES - N_GROUPS, d), F32)
        wr_t = jnp.concatenate(
            [router_group_w[l].T, pad,
             router_expert_w[l].transpose(0, 2, 1).reshape(N_EXPERTS, d)], axis=0).astype(BF16)
        br_t = jnp.concatenate(
            [router_group_b[l], jnp.zeros((SUBLANES - N_GROUPS,), F32),
             router_expert_b[l].reshape(-1)]).reshape(ROUTER_ROWS, 1)
        x2, h3, eidx, cw = _xattn_router_call(
            x, row(ln2_w[l]), xq_w[l].astype(BF16), kmem, vmem, xo_w[l].astype(BF16),
            row(ln3_w[l]), wr_t, br_t)

        n_blocks = (2 * t) // ROW_BLOCK + N_EXPERTS
        eidx_t = eidx.transpose(1, 0, 2).reshape(2, t)
        dest, block_expert, n_active, tail_start = _routing_plan(eidx_t, n_blocks)
        xs = _dispatch_call(tail_start, n_active, dest[0], dest[1], h3.reshape(t, d),
                            n_blocks * ROW_BLOCK)
        ys = _experts_call(block_expert, n_active, xs, expert_gate_w[l], expert_up_w[l],
                           expert_down_w[l])
        cw_t = cw.transpose(0, 2, 1).reshape(t, 2)
        is_last = l == depth - 1
        assert is_last, "only a single layer is supported"
        x = _combine_call(dest[0], dest[1], x2.reshape(t, d), cw_t, row(final_norm_w), ys)
        x = x.reshape(b, s, d)
    return x
```

```python
import functools

import jax
import jax.numpy as jnp
import numpy as np
from jax import lax
from jax.experimental import pallas as pl
from jax.experimental.pallas import tpu as pltpu

F32 = jnp.float32
BF16 = jnp.bfloat16

D_MODEL = 1024
D_RNN = 512
D_ATT = 512
N_RNN_BLOCKS = 8
CONV_WIDTH = 4
RG_LRU_C = 8.0
N_ATT_HEADS = 8
ATT_HEAD_DIM = 64
CHUNK = 64
LEFT_CHUNKS = 8
BAND = (LEFT_CHUNKS + 1) * CHUNK
PAIR_ROWS = 2 * CHUNK
PAIR_KEYS = BAND + CHUNK
REL_CLIP = 128
N_XHEADS = 4
XHEAD_DIM = 256
N_GROUPS = 4
EXPERTS_PER_GROUP = 8
N_EXPERTS = 32
D_EXPERT = 512
EPS = 1e-6
NEG_INF = -1e30

SUBLANES = 8
SEQ_TILE = LEFT_CHUNKS * CHUNK
TOK_TILE = 512
ROW_BLOCK = 512
MOVE_TILE = 256
ROUTER_ROWS = 40
VMEM_LIMIT = 56 * 1024 * 1024


def _rms(x, w):
    return x * lax.rsqrt(jnp.mean(x * x, axis=-1, keepdims=True) + EPS) * w


def _const_spec(shape):
    return pl.BlockSpec(shape, lambda *_: (0,) * len(shape))


def _memkv_kernel(mem_ref, w_ref, wk_ref, wv_ref, k_ref, v_ref):
    mn = _rms(mem_ref[...], w_ref[...]).astype(BF16)
    k_ref[...] = jnp.dot(mn, wk_ref[...], preferred_element_type=F32).astype(BF16)
    v_ref[...] = jnp.dot(mn, wv_ref[...], preferred_element_type=F32).astype(BF16)


def _memkv_call(mem, mem_norm_w, wk, wv):
    b, m, d = mem.shape
    return pl.pallas_call(
        _memkv_kernel,
        grid=(b,),
        in_specs=[
            pl.BlockSpec((None, m, d), lambda i: (i, 0, 0)),
            _const_spec((1, d)),
            _const_spec((d, d)),
            _const_spec((d, d)),
        ],
        out_specs=[
            pl.BlockSpec((None, m, d), lambda i: (i, 0, 0)),
            pl.BlockSpec((None, m, d), lambda i: (i, 0, 0)),
        ],
        out_shape=[jax.ShapeDtypeStruct((b, m, d), BF16)] * 2,
        compiler_params=pltpu.CompilerParams(vmem_limit_bytes=VMEM_LIMIT),
        name="mem_kv",
    )(mem, mem_norm_w, wk, wv)


def _mixer_kernel(x_ref, ln1_ref, win_ref, wkt_ref, convw_ref, convb_ref, wa_ref, ba_ref, wx_ref,
                  bx_ref, lam_ref, bias_ref, gnr_ref, gna_ref, wout_ref, o_ref,
                  xr_ext, h_state, q_s, kt_s, vbuf, yatt):
    ts = SEQ_TILE
    i = pl.program_id(1)

    @pl.when(i == 0)
    def _():
        xr_ext[pl.ds(0, SUBLANES), :] = jnp.zeros((SUBLANES, D_RNN), F32)
        h_state[...] = jnp.zeros_like(h_state)
        kt_s[:, pl.ds(0, ts)] = jnp.zeros((D_ATT, ts), BF16)
        vbuf[pl.ds(0, ts), :] = jnp.zeros((ts, D_ATT), BF16)

    x = x_ref[...]
    h = _rms(x, ln1_ref[...]).astype(BF16)

    def proj(lo):
        return jnp.dot(h, win_ref[:, lo:lo + D_RNN], preferred_element_type=F32)

    xr_ext[pl.ds(SUBLANES, ts), :] = proj(0)
    xc = convb_ref[...] + sum(
        convw_ref[j:j + 1, :] * xr_ext[pl.ds(SUBLANES - (CONV_WIDTH - 1) + j, ts), :]
        for j in range(CONV_WIDTH))
    xr_ext[pl.ds(0, SUBLANES), :] = xr_ext[pl.ds(ts, SUBLANES), :]
    xcb = xc.astype(BF16)
    r = jax.nn.sigmoid(jnp.dot(xcb, wa_ref[...], preferred_element_type=F32) + ba_ref[...])
    gate_i = jax.nn.sigmoid(jnp.dot(xcb, wx_ref[...], preferred_element_type=F32) + bx_ref[...])
    z = -lam_ref[...]
    softplus = jnp.maximum(z, 0.0) + jnp.log1p(jnp.exp(-jnp.abs(z)))
    log_a = (-RG_LRU_C * r) * softplus
    a = jnp.exp(log_a)
    t = jnp.tanh(log_a)
    u = jnp.sqrt(-2.0 * t / (1.0 - t)) * (gate_i * xc)

    ng = ts // SUBLANES
    a3 = a.reshape(ng, SUBLANES, D_RNN)
    b3 = u.reshape(ng, SUBLANES, D_RNN)
    row = lax.broadcasted_iota(jnp.int32, (ng, SUBLANES, D_RNN), 1)
    for d in (1, 2, 4):
        a_sh = pltpu.roll(a3, d, 1)
        b_sh = pltpu.roll(b3, d, 1)
        m = row >= d
        b3 = jnp.where(m, a3 * b_sh + b3, b3)
        a3 = jnp.where(m, a3 * a_sh, a3)

    hprev = h_state[...]
    hs = []
    for g in range(ng):
        hg = a3[g] * hprev + b3[g]
        hs.append(hg)
        hprev = jnp.broadcast_to(hg[SUBLANES - 1:SUBLANES, :], (SUBLANES, D_RNN))
    h_state[...] = hprev
    hseq = jnp.concatenate(hs, axis=0)
    xg = proj(D_RNN)
    gelu = 0.5 * xg * (1.0 + jnp.tanh(0.7978845608028654 * (xg + 0.044715 * (xg * xg * xg))))
    y_rnn = _rms(gelu * hseq, gnr_ref[...]).astype(BF16)

    q_s[...] = (proj(2 * D_RNN) * (ATT_HEAD_DIM ** -0.5)).astype(BF16)
    kt_s[:, pl.ds(ts, ts)] = lax.dot_general(
        wkt_ref[...], h, (((1,), (1,)), ((), ())), preferred_element_type=F32).astype(BF16)
    vbuf[pl.ds(ts, ts), :] = proj(2 * D_RNN + 2 * D_ATT).astype(BF16)
    lane = lax.broadcasted_iota(jnp.int32, (PAIR_ROWS, PAIR_ROWS), 1)
    first_head = lane < ATT_HEAD_DIM
    key_pos = lax.broadcasted_iota(jnp.int32, (PAIR_ROWS, PAIR_KEYS), 1)
    zero_q = jnp.zeros((PAIR_ROWS, PAIR_ROWS), BF16)

    for jj in range(ts // PAIR_ROWS):
        w0 = jj * PAIR_ROWS
        valid = key_pos >= jnp.where(i > 0, 0, ts - w0)
        scores = []
        for hd in range(N_ATT_HEADS):
            fl = slice((hd // 2) * PAIR_ROWS, (hd // 2 + 1) * PAIR_ROWS)
            q2 = q_s[pl.ds(w0, PAIR_ROWS), fl]
            qm = jnp.where(first_head if hd % 2 == 0 else ~first_head, q2, zero_q)
            scores.append(jnp.dot(qm, kt_s[fl, pl.ds(w0, PAIR_KEYS)], preferred_element_type=F32))
        outs = []
        for hp in range(N_ATT_HEADS // 2):
            fl = slice(hp * PAIR_ROWS, (hp + 1) * PAIR_ROWS)
            v2 = vbuf[pl.ds(w0, PAIR_KEYS), fl]
            o2 = []
            for hd in (2 * hp, 2 * hp + 1):
                s = jnp.where(valid, scores[hd] + bias_ref[hd], NEG_INF)
                p = jnp.exp(s - jnp.max(s, axis=-1, keepdims=True))
                l = jnp.sum(p, axis=-1, keepdims=True)
                o2.append(jnp.dot(p.astype(BF16), v2, preferred_element_type=F32) / l)
            outs.append(jnp.where(first_head, o2[0], o2[1]))
        yatt[pl.ds(w0, PAIR_ROWS), :] = jnp.concatenate(outs, axis=-1)

    kt_s[:, pl.ds(0, ts)] = kt_s[:, pl.ds(ts, ts)]
    vbuf[pl.ds(0, ts), :] = vbuf[pl.ds(ts, ts), :]
    y_att = _rms(yatt[...], gna_ref[...]).astype(BF16)

    mix = (jnp.dot(y_rnn, wout_ref[pl.ds(0, D_RNN), :], preferred_element_type=F32)
           + jnp.dot(y_att, wout_ref[pl.ds(D_RNN, D_ATT), :], preferred_element_type=F32))
    o_ref[...] = x + mix


def _mixer_call(x, ln1_w, w_in, wk_t, conv_w, conv_b, wa_bd, ba, wx_bd, bx, lam, bias_band, gnr, gna,
                w_out):
    b, s, d = x.shape
    ts = SEQ_TILE
    assert s % ts == 0
    dp = w_in.shape[1]
    return pl.pallas_call(
        _mixer_kernel,
        grid=(b, s // ts),
        in_specs=[
            pl.BlockSpec((None, ts, d), lambda bi, i: (bi, i, 0)),
            _const_spec((1, d)),
            _const_spec((d, dp)),
            _const_spec((D_ATT, d)),
            _const_spec((CONV_WIDTH, D_RNN)),
            _const_spec((1, D_RNN)),
            _const_spec((D_RNN, D_RNN)),
            _const_spec((1, D_RNN)),
            _const_spec((D_RNN, D_RNN)),
            _const_spec((1, D_RNN)),
            _const_spec((1, D_RNN)),
            _const_spec((N_ATT_HEADS, PAIR_ROWS, PAIR_KEYS)),
            _const_spec((1, D_RNN)),
            _const_spec((1, D_ATT)),
            _const_spec((d, d)),
        ],
        out_specs=pl.BlockSpec((None, ts, d), lambda bi, i: (bi, i, 0)),
        out_shape=jax.ShapeDtypeStruct((b, s, d), F32),
        scratch_shapes=[
            pltpu.VMEM((ts + SUBLANES, D_RNN), F32),
            pltpu.VMEM((SUBLANES, D_RNN), F32),
            pltpu.VMEM((ts, D_ATT), BF16),
            pltpu.VMEM((D_ATT, 2 * ts), BF16),
            pltpu.VMEM((2 * ts, D_ATT), BF16),
            pltpu.VMEM((ts, D_ATT), F32),
        ],
        compiler_params=pltpu.CompilerParams(
            dimension_semantics=("arbitrary", "arbitrary"), vmem_limit_bytes=VMEM_LIMIT),
        name="mixer",
    )(x, ln1_w, w_in, wk_t, conv_w, conv_b, wa_bd, ba, wx_bd, bx, lam, bias_band, gnr, gna, w_out)


def _xattn_router_kernel(x_ref, ln2_ref, wq_ref, k_ref, v_ref, wo_ref, ln3_ref, wr_ref, br_ref,
                         x2_ref, h3_ref, eidx_ref, cw_ref):
    x = x_ref[...]
    h = _rms(x, ln2_ref[...]).astype(BF16)
    q = (jnp.dot(h, wq_ref[...], preferred_element_type=F32) * (XHEAD_DIM ** -0.5)).astype(BF16)
    outs = []
    for hd in range(N_XHEADS):
        sl = slice(hd * XHEAD_DIM, (hd + 1) * XHEAD_DIM)
        s = lax.dot_general(q[:, sl], k_ref[:, sl], (((1,), (1,)), ((), ())),
                            preferred_element_type=F32)
        p = jnp.exp(s - jnp.max(s, axis=-1, keepdims=True))
        l = jnp.sum(p, axis=-1, keepdims=True)
        o = jnp.dot(p.astype(BF16), v_ref[:, sl], preferred_element_type=F32)
        outs.append((o / l).astype(BF16))
    att = jnp.concatenate(outs, axis=-1)
    x2 = x + jnp.dot(att, wo_ref[...], preferred_element_type=F32)
    x2_ref[...] = x2

    h3 = _rms(x2, ln3_ref[...])
    h3_ref[...] = h3
    logits = lax.dot_general(wr_ref[...], h3.astype(BF16), (((1,), (1,)), ((), ())),
                             preferred_element_type=F32) + br_ref[...]
    gl = logits[0:N_GROUPS, :]
    ge = jnp.exp(gl - jnp.max(gl, axis=0, keepdims=True))
    gp = ge / jnp.sum(ge, axis=0, keepdims=True)
    g_p = jnp.max(gp, axis=0, keepdims=True)
    g_iota = lax.broadcasted_iota(jnp.int32, gp.shape, 0)
    g_idx = jnp.min(jnp.where(gp == g_p, g_iota, N_GROUPS), axis=0, keepdims=True)

    el = jnp.zeros((EXPERTS_PER_GROUP, logits.shape[1]), F32)
    for g in range(N_GROUPS):
        lo = SUBLANES + g * EXPERTS_PER_GROUP
        el = jnp.where(g_idx == g, logits[lo:lo + EXPERTS_PER_GROUP, :], el)
    ee = jnp.exp(el - jnp.max(el, axis=0, keepdims=True))
    ep = ee / jnp.sum(ee, axis=0, keepdims=True)
    e_iota = lax.broadcasted_iota(jnp.int32, ep.shape, 0)
    p1 = jnp.max(ep, axis=0, keepdims=True)
    i1 = jnp.min(jnp.where(ep == p1, e_iota, EXPERTS_PER_GROUP), axis=0, keepdims=True)
    ep2 = jnp.where(e_iota == i1, -1.0, ep)
    p2 = jnp.max(ep2, axis=0, keepdims=True)
    i2 = jnp.min(jnp.where(ep2 == p2, e_iota, EXPERTS_PER_GROUP), axis=0, keepdims=True)
    den = p1 + p2
    eidx_ref[...] = jnp.concatenate(
        [g_idx * EXPERTS_PER_GROUP + i1, g_idx * EXPERTS_PER_GROUP + i2], axis=0)
    cw_ref[...] = jnp.concatenate([g_p * (p1 / den), g_p * (p2 / den)], axis=0)


def _xattn_router_call(x1, ln2_w, wq, kmem, vmem, wo, ln3_w, wr_t, br_t):
    b, s, d = x1.shape
    tt = TOK_TILE
    m = kmem.shape[1]
    return pl.pallas_call(
        _xattn_router_kernel,
        grid=(b, s // tt),
        in_specs=[
            pl.BlockSpec((None, tt, d), lambda bi, i: (bi, i, 0)),
            _const_spec((1, d)),
            _const_spec((d, d)),
            pl.BlockSpec((None, m, d), lambda bi, i: (bi, 0, 0)),
            pl.BlockSpec((None, m, d), lambda bi, i: (bi, 0, 0)),
            _const_spec((d, d)),
            _const_spec((1, d)),
            _const_spec((ROUTER_ROWS, d)),
            _const_spec((ROUTER_ROWS, 1)),
        ],
        out_specs=[
            pl.BlockSpec((None, tt, d), lambda bi, i: (bi, i, 0)),
            pl.BlockSpec((None, tt, d), lambda bi, i: (bi, i, 0)),
            pl.BlockSpec((None, 2, tt), lambda bi, i: (bi, 0, i)),
            pl.BlockSpec((None, 2, tt), lambda bi, i: (bi, 0, i)),
        ],
        out_shape=[
            jax.ShapeDtypeStruct((b, s, d), F32),
            jax.ShapeDtypeStruct((b, s, d), F32),
            jax.ShapeDtypeStruct((b, 2, s), jnp.int32),
            jax.ShapeDtypeStruct((b, 2, s), F32),
        ],
        compiler_params=pltpu.CompilerParams(
            dimension_semantics=("arbitrary", "arbitrary"), vmem_limit_bytes=VMEM_LIMIT),
        name="xattn_router",
    )(x1, ln2_w, wq, kmem, vmem, wo, ln3_w, wr_t, br_t)


def _dispatch_kernel(tail_ref, na_ref, d0_ref, d1_ref, h_ref, xs_ref, zbuf, sem, zsem):
    i = pl.program_id(0)
    tm = MOVE_TILE
    nb = xs_ref.shape[0] // ROW_BLOCK

    @pl.when(i == 0)
    def _():
        zbuf[...] = jnp.zeros_like(zbuf)

        def zero_copy(start_row):
            start_row = pl.multiple_of(start_row, ROW_BLOCK)
            return pltpu.make_async_copy(zbuf, xs_ref.at[pl.ds(start_row, ROW_BLOCK), :], zsem)

        def start(e, c):
            @pl.when(tail_ref[e] >= 0)
            def _():
                zero_copy(tail_ref[e]).start()
            return c

        def wait(e, c):
            @pl.when(tail_ref[e] >= 0)
            def _():
                zero_copy(tail_ref[e]).wait()
            return c

        def start_unused(b, c):
            zero_copy(b * ROW_BLOCK).start()
            return c

        def wait_unused(b, c):
            zero_copy(b * ROW_BLOCK).wait()
            return c

        lax.fori_loop(0, N_EXPERTS, start, 0)
        lax.fori_loop(na_ref[0], nb, start_unused, 0)
        lax.fori_loop(0, N_EXPERTS, wait, 0)
        lax.fori_loop(na_ref[0], nb, wait_unused, 0)

    def row_copy(t, dst_row):
        return pltpu.make_async_copy(h_ref.at[pl.ds(t, 1), :], xs_ref.at[pl.ds(dst_row, 1), :], sem)

    def start(t, c):
        row_copy(t, d0_ref[0, t]).start()
        row_copy(t, d1_ref[0, t]).start()
        return c

    def wait(t, c):
        row_copy(t, 0).wait()
        row_copy(t, 0).wait()
        return c

    lax.fori_loop(0, tm, start, 0, unroll=8)
    lax.fori_loop(0, tm, wait, 0, unroll=8)


def _dispatch_call(tail_start, n_active, dest0, dest1, h3, n_rows):
    t, d = h3.shape
    tm = MOVE_TILE
    nt = t // tm
    grid_spec = pltpu.PrefetchScalarGridSpec(
        num_scalar_prefetch=2,
        grid=(nt,),
        in_specs=[
            pl.BlockSpec((None, 1, tm), lambda i, *_: (i, 0, 0), memory_space=pltpu.SMEM),
            pl.BlockSpec((None, 1, tm), lambda i, *_: (i, 0, 0), memory_space=pltpu.SMEM),
            pl.BlockSpec((tm, d), lambda i, *_: (i, 0)),
        ],
        out_specs=pl.BlockSpec(memory_space=pl.ANY),
        scratch_shapes=[
            pltpu.VMEM((ROW_BLOCK, d), F32),
            pltpu.SemaphoreType.DMA(()),
            pltpu.SemaphoreType.DMA(()),
        ],
    )
    return pl.pallas_call(
        _dispatch_kernel,
        grid_spec=grid_spec,
        out_shape=jax.ShapeDtypeStruct((n_rows, d), F32),
        compiler_params=pltpu.CompilerParams(
            dimension_semantics=("arbitrary",), vmem_limit_bytes=VMEM_LIMIT,
            has_side_effects=True),
        name="dispatch",
    )(tail_start, n_active, dest0.reshape(nt, 1, tm), dest1.reshape(nt, 1, tm), h3)


def _experts_kernel(be_ref, na_ref, xs_ref, wg_ref, wu_ref, wd_ref, ys_ref, wg_s, wu_s, wd_s, prev_e):
    b = pl.program_id(0)

    @pl.when(b == 0)
    def _():
        prev_e[0] = -1

    @pl.when(b < na_ref[0])
    def _():
        e = be_ref[b]

        @pl.when(e != prev_e[0])
        def _():
            wg_s[...] = wg_ref[...].astype(BF16)
            wu_s[...] = wu_ref[...].astype(BF16)
            wd_s[...] = wd_ref[...].astype(BF16)
            prev_e[0] = e

        xb = xs_ref[...].astype(BF16)
        g = jnp.dot(xb, wg_s[...], preferred_element_type=F32)
        u = jnp.dot(xb, wu_s[...], preferred_element_type=F32)
        hmid = (g * jax.nn.sigmoid(g) * u).astype(BF16)
        ys_ref[...] = jnp.dot(hmid, wd_s[...], preferred_element_type=F32)

    @pl.when(b >= na_ref[0])
    def _():
        ys_ref[...] = jnp.zeros_like(ys_ref)


def _experts_call(block_expert, n_active, xs, w_gate, w_up, w_down):
    n_rows, d = xs.shape
    nb = n_rows // ROW_BLOCK
    de = w_gate.shape[-1]

    def blk(b, be, na):
        return jnp.minimum(b, na[0] - 1)

    grid_spec = pltpu.PrefetchScalarGridSpec(
        num_scalar_prefetch=2,
        grid=(nb,),
        in_specs=[
            pl.BlockSpec((ROW_BLOCK, d), lambda b, be, na: (blk(b, be, na), 0)),
            pl.BlockSpec((None, d, de), lambda b, be, na: (be[blk(b, be, na)], 0, 0)),
            pl.BlockSpec((None, d, de), lambda b, be, na: (be[blk(b, be, na)], 0, 0)),
            pl.BlockSpec((None, de, d), lambda b, be, na: (be[blk(b, be, na)], 0, 0)),
        ],
        out_specs=pl.BlockSpec((ROW_BLOCK, d), lambda b, be, na: (b, 0)),
        scratch_shapes=[
            pltpu.VMEM((d, de), BF16),
            pltpu.VMEM((d, de), BF16),
            pltpu.VMEM((de, d), BF16),
            pltpu.SMEM((1,), jnp.int32),
        ],
    )
    return pl.pallas_call(
        _experts_kernel,
        grid_spec=grid_spec,
        out_shape=jax.ShapeDtypeStruct((n_rows, d), F32),
        compiler_params=pltpu.CompilerParams(
            dimension_semantics=("arbitrary",), vmem_limit_bytes=VMEM_LIMIT),
        name="experts",
    )(block_expert, n_active, xs, w_gate, w_up, w_down)


def _combine_kernel(d0_ref, d1_ref, x_ref, cw_ref, fw_ref, ys_ref, o_ref, y0, y1, sem):
    tm = MOVE_TILE

    def row_copy(src_row, buf, t):
        return pltpu.make_async_copy(ys_ref.at[pl.ds(src_row, 1), :], buf.at[pl.ds(t, 1), :], sem)

    def start(t, c):
        row_copy(d0_ref[0, t], y0, t).start()
        row_copy(d1_ref[0, t], y1, t).start()
        return c

    def wait(t, c):
        row_copy(0, y0, t).wait()
        row_copy(0, y1, t).wait()
        return c

    lax.fori_loop(0, tm, start, 0, unroll=8)
    lax.fori_loop(0, tm, wait, 0, unroll=8)
    cw = cw_ref[...]
    moe = cw[:, 0:1] * y0[...] + cw[:, 1:2] * y1[...]
    o_ref[...] = _rms(x_ref[...] + moe, fw_ref[...])


def _combine_call(dest0, dest1, x2, cw_t, final_w, ys):
    t, d = x2.shape
    tm = MOVE_TILE
    nt = t // tm
    return pl.pallas_call(
        _combine_kernel,
        grid=(nt,),
        in_specs=[
            pl.BlockSpec((None, 1, tm), lambda i: (i, 0, 0), memory_space=pltpu.SMEM),
            pl.BlockSpec((None, 1, tm), lambda i: (i, 0, 0), memory_space=pltpu.SMEM),
            pl.BlockSpec((tm, d), lambda i: (i, 0)),
            pl.BlockSpec((tm, 2), lambda i: (i, 0)),
            _const_spec((1, d)),
            pl.BlockSpec(memory_space=pl.ANY),
        ],
        out_specs=pl.BlockSpec((tm, d), lambda i: (i, 0)),
        out_shape=jax.ShapeDtypeStruct((t, d), F32),
        scratch_shapes=[
            pltpu.VMEM((tm, d), F32),
            pltpu.VMEM((tm, d), F32),
            pltpu.SemaphoreType.DMA(()),
        ],
        compiler_params=pltpu.CompilerParams(
            dimension_semantics=("arbitrary",), vmem_limit_bytes=VMEM_LIMIT),
        name="combine",
    )(dest0.reshape(nt, 1, tm), dest1.reshape(nt, 1, tm), x2, cw_t, final_w, ys)


def _block_diag(w):
    n, c, d = w.shape
    eye = jnp.eye(n, dtype=w.dtype)
    return (w[:, :, None, :] * eye[:, None, :, None]).reshape(n * c, n * d)


def _rel_bias_pair(rel_bias):
    nh = rel_bias.shape[0]
    far = LEFT_CHUNKS * CHUNK - REL_CLIP
    t = jnp.concatenate(
        [rel_bias[:, REL_CLIP - (CHUNK - 1):],
         jnp.broadcast_to(rel_bias[:, -1:], (nh, far + CHUNK - 1))], axis=1)
    band = jnp.stack([t[:, qi:qi + BAND] for qi in range(CHUNK)], axis=1)[:, :, ::-1]
    neg = jnp.full((nh, CHUNK, CHUNK), NEG_INF, F32)
    return jnp.concatenate(
        [jnp.concatenate([band, neg], axis=2), jnp.concatenate([neg, band], axis=2)], axis=1)


def _routing_plan(eidx, n_blocks):
    flat_e = eidx.reshape(-1)
    onehot = (flat_e[:, None] == jnp.arange(N_EXPERTS, dtype=jnp.int32)[None, :]).astype(jnp.int32)
    csum = jnp.cumsum(onehot, axis=0)
    counts = csum[-1]
    rank = jnp.sum(onehot * csum, axis=1) - 1
    padded = (counts + ROW_BLOCK - 1) // ROW_BLOCK * ROW_BLOCK
    pad_end = jnp.cumsum(padded)
    pad_start = pad_end - padded
    dest = jnp.sum(onehot * pad_start[None, :], axis=1) + rank
    block_row = jnp.arange(n_blocks, dtype=jnp.int32) * ROW_BLOCK
    block_expert = jnp.minimum(
        jnp.sum((pad_end[None, :] <= block_row[:, None]).astype(jnp.int32), axis=1), N_EXPERTS - 1)
    n_active = (pad_end[-1] // ROW_BLOCK).astype(jnp.int32).reshape(1)
    tail_start = jnp.where(padded > 0, pad_end - ROW_BLOCK, -1).astype(jnp.int32)
    return dest.astype(jnp.int32).reshape(eidx.shape), block_expert, n_active, tail_start


def kernel(x, mem, ln1_w, w_in, conv_w, conv_b, rnn_wa, rnn_ba, rnn_wx, rnn_bx, rnn_lambda, rel_bias, gn_rnn_w, gn_att_w, w_out, ln2_w, mem_norm_w, xq_w, xk_w, xv_w, xo_w, ln3_w, router_group_w, router_group_b, router_expert_w, router_expert_b, expert_gate_w, expert_up_w, expert_down_w, final_norm_w):
    b, s, d = x.shape
    t = b * s
    depth = ln1_w.shape[0]
    row = lambda v: v.reshape(1, -1)

    kmem, vmem = None, None
    for l in range(depth):
        x = _mixer_call(
            x, row(ln1_w[l]), w_in[l].astype(BF16),
            w_in[l][:, 2 * D_RNN + D_ATT:2 * D_RNN + 2 * D_ATT].T.astype(BF16),
            conv_w[l], row(conv_b[l]),
            _block_diag(rnn_wa[l]).astype(BF16), row(rnn_ba[l]),
            _block_diag(rnn_wx[l]).astype(BF16), row(rnn_bx[l]), row(rnn_lambda[l]),
            _rel_bias_pair(rel_bias[l]), row(gn_rnn_w[l]), row(gn_att_w[l]), w_out[l].astype(BF16))

        kmem, vmem = _memkv_call(mem, row(mem_norm_w), xk_w[l].astype(BF16), xv_w[l].astype(BF16))
        pad = jnp.zeros((SUBLANES - N_GROUPS, d), F32)
        wr_t = jnp.concatenate(
            [router_group_w[l].T, pad,
             router_expert_w[l].transpose(0, 2, 1).reshape(N_EXPERTS, d)], axis=0).astype(BF16)
        br_t = jnp.concatenate(
            [router_group_b[l], jnp.zeros((SUBLANES - N_GROUPS,), F32),
             router_expert_b[l].reshape(-1)]).reshape(ROUTER_ROWS, 1)
        x2, h3, eidx, cw = _xattn_router_call(
            x, row(ln2_w[l]), xq_w[l].astype(BF16), kmem, vmem, xo_w[l].astype(BF16),
            row(ln3_w[l]), wr_t, br_t)

        n_blocks = (2 * t) // ROW_BLOCK + N_EXPERTS
        eidx_t = eidx.transpose(1, 0, 2).reshape(2, t)
        dest, block_expert, n_active, tail_start = _routing_plan(eidx_t, n_blocks)
        xs = _dispatch_call(tail_start, n_active, dest[0], dest[1], h3.reshape(t, d),
                            n_blocks * ROW_BLOCK)
        ys = _experts_call(block_expert, n_active, xs, expert_gate_w[l], expert_up_w[l],
                           expert_down_w[l])
        cw_t = cw.transpose(0, 2, 1).reshape(t, 2)
        is_last = l == depth - 1
        assert is_last, "only a single layer is supported"
        x = _combine_call(dest[0], dest[1], x2.reshape(t, d), cw_t, row(final_norm_w), ys)
        x = x.reshape(b, s, d)
    return x
```

```python
import functools

import jax
import jax.numpy as jnp
import numpy as np
from jax import lax
from jax.experimental import pallas as pl
from jax.experimental.pallas import tpu as pltpu

F32 = jnp.float32
BF16 = jnp.bfloat16

D_MODEL = 1024
D_RNN = 512
D_ATT = 512
N_RNN_BLOCKS = 8
CONV_WIDTH = 4
RG_LRU_C = 8.0
N_ATT_HEADS = 8
ATT_HEAD_DIM = 64
CHUNK = 64
LEFT_CHUNKS = 8
BAND = (LEFT_CHUNKS + 1) * CHUNK
PAIR_ROWS = 2 * CHUNK
PAIR_KEYS = BAND + CHUNK
REL_CLIP = 128
N_XHEADS = 4
XHEAD_DIM = 256
N_GROUPS = 4
EXPERTS_PER_GROUP = 8
N_EXPERTS = 32
D_EXPERT = 512
EPS = 1e-6
NEG_INF = -1e30

SUBLANES = 8
SEQ_TILE = LEFT_CHUNKS * CHUNK
TOK_TILE = 512
ROW_BLOCK = 512
MOVE_TILE = 256
ROUTER_ROWS = 40
VMEM_LIMIT = 56 * 1024 * 1024


def _rms(x, w):
    return x * lax.rsqrt(jnp.mean(x * x, axis=-1, keepdims=True) + EPS) * w


def _const_spec(shape):
    return pl.BlockSpec(shape, lambda *_: (0,) * len(shape))


def _memkv_kernel(mem_ref, w_ref, wk_ref, wv_ref, k_ref, v_ref):
    mn = _rms(mem_ref[...], w_ref[...]).astype(BF16)
    k_ref[...] = jnp.dot(mn, wk_ref[...], preferred_element_type=F32).astype(BF16)
    v_ref[...] = jnp.dot(mn, wv_ref[...], preferred_element_type=F32).astype(BF16)


def _memkv_call(mem, mem_norm_w, wk, wv):
    b, m, d = mem.shape
    return pl.pallas_call(
        _memkv_kernel,
        grid=(b,),
        in_specs=[
            pl.BlockSpec((None, m, d), lambda i: (i, 0, 0)),
            _const_spec((1, d)),
            _const_spec((d, d)),
            _const_spec((d, d)),
        ],
        out_specs=[
            pl.BlockSpec((None, m, d), lambda i: (i, 0, 0)),
            pl.BlockSpec((None, m, d), lambda i: (i, 0, 0)),
        ],
        out_shape=[jax.ShapeDtypeStruct((b, m, d), BF16)] * 2,
        compiler_params=pltpu.CompilerParams(vmem_limit_bytes=VMEM_LIMIT),
        name="mem_kv",
    )(mem, mem_norm_w, wk, wv)


def _mixer_kernel(x_ref, ln1_ref, win_ref, wkt_ref, convw_ref, convb_ref, wa_ref, ba_ref, wx_ref,
                  bx_ref, lam_ref, bias_ref, gnr_ref, gna_ref, wout_ref, o_ref,
                  xr_ext, h_state, q_s, kt_s, vbuf, yatt):
    ts = SEQ_TILE
    i = pl.program_id(1)

    @pl.when(i == 0)
    def _():
        xr_ext[pl.ds(0, SUBLANES), :] = jnp.zeros((SUBLANES, D_RNN), F32)
        h_state[...] = jnp.zeros_like(h_state)
        kt_s[:, pl.ds(0, ts)] = jnp.zeros((D_ATT, ts), BF16)
        vbuf[pl.ds(0, ts), :] = jnp.zeros((ts, D_ATT), BF16)

    x = x_ref[...]
    h = _rms(x, ln1_ref[...]).astype(BF16)

    def proj(lo):
        return jnp.dot(h, win_ref[:, lo:lo + D_RNN], preferred_element_type=F32)

    xr_ext[pl.ds(SUBLANES, ts), :] = proj(0)
    xc = convb_ref[...] + sum(
        convw_ref[j:j + 1, :] * xr_ext[pl.ds(SUBLANES - (CONV_WIDTH - 1) + j, ts), :]
        for j in range(CONV_WIDTH))
    xr_ext[pl.ds(0, SUBLANES), :] = xr_ext[pl.ds(ts, SUBLANES), :]
    xcb = xc.astype(BF16)
    r = jax.nn.sigmoid(jnp.dot(xcb, wa_ref[...], preferred_element_type=F32) + ba_ref[...])
    gate_i = jax.nn.sigmoid(jnp.dot(xcb, wx_ref[...], preferred_element_type=F32) + bx_ref[...])
    z = -lam_ref[...]
    softplus = jnp.maximum(z, 0.0) + jnp.log1p(jnp.exp(-jnp.abs(z)))
    log_a = (-RG_LRU_C * r) * softplus
    a = jnp.exp(log_a)
    t = jnp.tanh(log_a)
    u = jnp.sqrt(-2.0 * t / (1.0 - t)) * (gate_i * xc)

    ng = ts // SUBLANES
    a3 = a.reshape(ng, SUBLANES, D_RNN)
    b3 = u.reshape(ng, SUBLANES, D_RNN)
    row = lax.broadcasted_iota(jnp.int32, (ng, SUBLANES, D_RNN), 1)
    for d in (1, 2, 4):
        a_sh = pltpu.roll(a3, d, 1)
        b_sh = pltpu.roll(b3, d, 1)
        m = row >= d
        b3 = jnp.where(m, a3 * b_sh + b3, b3)
        a3 = jnp.where(m, a3 * a_sh, a3)

    hprev = h_state[...]
    hs = []
    for g in range(ng):
        hg = a3[g] * hprev + b3[g]
        hs.append(hg)
        hprev = jnp.broadcast_to(hg[SUBLANES - 1:SUBLANES, :], (SUBLANES, D_RNN))
    h_state[...] = hprev
    hseq = jnp.concatenate(hs, axis=0)
    xg = proj(D_RNN)
    gelu = 0.5 * xg * (1.0 + jnp.tanh(0.7978845608028654 * (xg + 0.044715 * (xg * xg * xg))))
    y_rnn = _rms(gelu * hseq, gnr_ref[...]).astype(BF16)

    q_s[...] = (proj(2 * D_RNN) * (ATT_HEAD_DIM ** -0.5)).astype(BF16)
    kt_s[:, pl.ds(ts, ts)] = lax.dot_general(
        wkt_ref[...], h, (((1,), (1,)), ((), ())), preferred_element_type=F32).astype(BF16)
    vbuf[pl.ds(ts, ts), :] = proj(2 * D_RNN + 2 * D_ATT).astype(BF16)
    lane = lax.broadcasted_iota(jnp.int32, (PAIR_ROWS, PAIR_ROWS), 1)
    first_head = lane < ATT_HEAD_DIM
    key_pos = lax.broadcasted_iota(jnp.int32, (PAIR_ROWS, PAIR_KEYS), 1)
    zero_q = jnp.zeros((PAIR_ROWS, PAIR_ROWS), BF16)

    for jj in range(ts // PAIR_ROWS):
        w0 = jj * PAIR_ROWS
        valid = key_pos >= jnp.where(i > 0, 0, ts - w0)
        scores = []
        for hd in range(N_ATT_HEADS):
            fl = slice((hd // 2) * PAIR_ROWS, (hd // 2 + 1) * PAIR_ROWS)
            q2 = q_s[pl.ds(w0, PAIR_ROWS), fl]
            qm = jnp.where(first_head if hd % 2 == 0 else ~first_head, q2, zero_q)
            scores.append(jnp.dot(qm, kt_s[fl, pl.ds(w0, PAIR_KEYS)], preferred_element_type=F32))
        outs = []
        for hp in range(N_ATT_HEADS // 2):
            fl = slice(hp * PAIR_ROWS, (hp + 1) * PAIR_ROWS)
            v2 = vbuf[pl.ds(w0, PAIR_KEYS), fl]
            o2 = []
            for hd in (2 * hp, 2 * hp + 1):
                s = jnp.where(valid, scores[hd] + bias_ref[hd], NEG_INF)
                p = jnp.exp(s - jnp.max(s, axis=-1, keepdims=True))
                l = jnp.sum(p, axis=-1, keepdims=True)
                o2.append(jnp.dot(p.astype(BF16), v2, preferred_element_type=F32) / l)
            outs.append(jnp.where(first_head, o2[0], o2[1]))
        yatt[pl.ds(w0, PAIR_ROWS), :] = jnp.concatenate(outs, axis=-1)

    kt_s[:, pl.ds(0, ts)] = kt_s[:, pl.ds(ts, ts)]
    vbuf[pl.ds(0, ts), :] = vbuf[pl.ds(ts, ts), :]
    y_att = _rms(yatt[...], gna_ref[...]).astype(BF16)

    mix = (jnp.dot(y_rnn, wout_ref[pl.ds(0, D_RNN), :], preferred_element_type=F32)
           + jnp.dot(y_att, wout_ref[pl.ds(D_RNN, D_ATT), :], preferred_element_type=F32))
    o_ref[...] = x + mix


def _mixer_call(x, ln1_w, w_in, wk_t, conv_w, conv_b, wa_bd, ba, wx_bd, bx, lam, bias_band, gnr, gna,
                w_out):
    b, s, d = x.shape
    ts = SEQ_TILE
    assert s % ts == 0
    dp = w_in.shape[1]
    return pl.pallas_call(
        _mixer_kernel,
        grid=(b, s // ts),
        in_specs=[
            pl.BlockSpec((None, ts, d), lambda bi, i: (bi, i, 0)),
            _const_spec((1, d)),
            _const_spec((d, dp)),
            _const_spec((D_ATT, d)),
            _const_spec((CONV_WIDTH, D_RNN)),
            _const_spec((1, D_RNN)),
            _const_spec((D_RNN, D_RNN)),
            _const_spec((1, D_RNN)),
            _const_spec((D_RNN, D_RNN)),
            _const_spec((1, D_RNN)),
            _const_spec((1, D_RNN)),
            _const_spec((N_ATT_HEADS, PAIR_ROWS, PAIR_KEYS)),
            _const_spec((1, D_RNN)),
            _const_spec((1, D_ATT)),
            _const_spec((d, d)),
        ],
        out_specs=pl.BlockSpec((None, ts, d), lambda bi, i: (bi, i, 0)),
        out_shape=jax.ShapeDtypeStruct((b, s, d), F32),
        scratch_shapes=[
            pltpu.VMEM((ts + SUBLANES, D_RNN), F32),
            pltpu.VMEM((SUBLANES, D_RNN), F32),
            pltpu.VMEM((ts, D_ATT), BF16),
            pltpu.VMEM((D_ATT, 2 * ts), BF16),
            pltpu.VMEM((2 * ts, D_ATT), BF16),
            pltpu.VMEM((ts, D_ATT), F32),
        ],
        compiler_params=pltpu.CompilerParams(
            dimension_semantics=("arbitrary", "arbitrary"), vmem_limit_bytes=VMEM_LIMIT),
        name="mixer",
    )(x, ln1_w, w_in, wk_t, conv_w, conv_b, wa_bd, ba, wx_bd, bx, lam, bias_band, gnr, gna, w_out)


def _xattn_router_kernel(x_ref, ln2_ref, wq_ref, k_ref, v_ref, wo_ref, ln3_ref, wr_ref, br_ref,
                         x2_ref, h3_ref, eidx_ref, cw_ref):
    x = x_ref[...]
    h = _rms(x, ln2_ref[...]).astype(BF16)
    q = (jnp.dot(h, wq_ref[...], preferred_element_type=F32) * (XHEAD_DIM ** -0.5)).astype(BF16)
    outs = []
    for hd in range(N_XHEADS):
        sl = slice(hd * XHEAD_DIM, (hd + 1) * XHEAD_DIM)
        s = lax.dot_general(q[:, sl], k_ref[:, sl], (((1,), (1,)), ((), ())),
                            preferred_element_type=F32)
        p = jnp.exp(s - jnp.max(s, axis=-1, keepdims=True))
        l = jnp.sum(p, axis=-1, keepdims=True)
        o = jnp.dot(p.astype(BF16), v_ref[:, sl], preferred_element_type=F32)
        outs.append((o / l).astype(BF16))
    att = jnp.concatenate(outs, axis=-1)
    x2 = x + jnp.dot(att, wo_ref[...], preferred_element_type=F32)
    x2_ref[...] = x2

    h3 = _rms(x2, ln3_ref[...])
    h3_ref[...] = h3
    logits = lax.dot_general(wr_ref[...], h3.astype(BF16), (((1,), (1,)), ((), ())),
                             preferred_element_type=F32) + br_ref[...]
    gl = logits[0:N_GROUPS, :]
    ge = jnp.exp(gl - jnp.max(gl, axis=0, keepdims=True))
    gp = ge / jnp.sum(ge, axis=0, keepdims=True)
    g_p = jnp.max(gp, axis=0, keepdims=True)
    g_iota = lax.broadcasted_iota(jnp.int32, gp.shape, 0)
    g_idx = jnp.min(jnp.where(gp == g_p, g_iota, N_GROUPS), axis=0, keepdims=True)

    el = jnp.zeros((EXPERTS_PER_GROUP, logits.shape[1]), F32)
    for g in range(N_GROUPS):
        lo = SUBLANES + g * EXPERTS_PER_GROUP
        el = jnp.where(g_idx == g, logits[lo:lo + EXPERTS_PER_GROUP, :], el)
    ee = jnp.exp(el - jnp.max(el, axis=0, keepdims=True))
    ep = ee / jnp.sum(ee, axis=0, keepdims=True)
    e_iota = lax.broadcasted_iota(jnp.int32, ep.shape, 0)
    p1 = jnp.max(ep, axis=0, keepdims=True)
    i1 = jnp.min(jnp.where(ep == p1, e_iota, EXPERTS_PER_GROUP), axis=0, keepdims=True)
    ep2 = jnp.where(e_iota == i1, -1.0, ep)
    p2 = jnp.max(ep2, axis=0, keepdims=True)
    i2 = jnp.min(jnp.where(ep2 == p2, e_iota, EXPERTS_PER_GROUP), axis=0, keepdims=True)
    den = p1 + p2
    eidx_ref[...] = jnp.concatenate(
        [g_idx * EXPERTS_PER_GROUP + i1, g_idx * EXPERTS_PER_GROUP + i2], axis=0)
    cw_ref[...] = jnp.concatenate([g_p * (p1 / den), g_p * (p2 / den)], axis=0)


def _xattn_router_call(x1, ln2_w, wq, kmem, vmem, wo, ln3_w, wr_t, br_t):
    b, s, d = x1.shape
    tt = TOK_TILE
    m = kmem.shape[1]
    return pl.pallas_call(
        _xattn_router_kernel,
        grid=(b, s // tt),
        in_specs=[
            pl.BlockSpec((None, tt, d), lambda bi, i: (bi, i, 0)),
            _const_spec((1, d)),
            _const_spec((d, d)),
            pl.BlockSpec((None, m, d), lambda bi, i: (bi, 0, 0)),
            pl.BlockSpec((None, m, d), lambda bi, i: (bi, 0, 0)),
            _const_spec((d, d)),
            _const_spec((1, d)),
            _const_spec((ROUTER_ROWS, d)),
            _const_spec((ROUTER_ROWS, 1)),
        ],
        out_specs=[
            pl.BlockSpec((None, tt, d), lambda bi, i: (bi, i, 0)),
            pl.BlockSpec((None, tt, d), lambda bi, i: (bi, i, 0)),
            pl.BlockSpec((None, 2, tt), lambda bi, i: (bi, 0, i)),
            pl.BlockSpec((None, 2, tt), lambda bi, i: (bi, 0, i)),
        ],
        out_shape=[
            jax.ShapeDtypeStruct((b, s, d), F32),
            jax.ShapeDtypeStruct((b, s, d), F32),
            jax.ShapeDtypeStruct((b, 2, s), jnp.int32),
            jax.ShapeDtypeStruct((b, 2, s), F32),
        ],
        compiler_params=pltpu.CompilerParams(
            dimension_semantics=("arbitrary", "arbitrary"), vmem_limit_bytes=VMEM_LIMIT),
        name="xattn_router",
    )(x1, ln2_w, wq, kmem, vmem, wo, ln3_w, wr_t, br_t)


def _dispatch_kernel(tail_ref, na_ref, d0_ref, d1_ref, h_ref, xs_ref, zbuf, sem, zsem):
    i = pl.program_id(0)
    tm = MOVE_TILE
    nb = xs_ref.shape[0] // ROW_BLOCK

    @pl.when(i == 0)
    def _():
        zbuf[...] = jnp.zeros_like(zbuf)

        def zero_copy(start_row):
            start_row = pl.multiple_of(start_row, ROW_BLOCK)
            return pltpu.make_async_copy(zbuf, xs_ref.at[pl.ds(start_row, ROW_BLOCK), :], zsem)

        def start(e, c):
            @pl.when(tail_ref[e] >= 0)
            def _():
                zero_copy(tail_ref[e]).start()
            return c

        def wait(e, c):
            @pl.when(tail_ref[e] >= 0)
            def _():
                zero_copy(tail_ref[e]).wait()
            return c

        def start_unused(b, c):
            zero_copy(b * ROW_BLOCK).start()
            return c

        def wait_unused(b, c):
            zero_copy(b * ROW_BLOCK).wait()
            return c

        lax.fori_loop(0, N_EXPERTS, start, 0)
        lax.fori_loop(na_ref[0], nb, start_unused, 0)
        lax.fori_loop(0, N_EXPERTS, wait, 0)
        lax.fori_loop(na_ref[0], nb, wait_unused, 0)

    def row_copy(t, dst_row):
        return pltpu.make_async_copy(h_ref.at[pl.ds(t, 1), :], xs_ref.at[pl.ds(dst_row, 1), :], sem)

    for t in range(tm):
        row_copy(t, d0_ref[0, t]).start(priority=0)
        row_copy(t, d1_ref[0, t]).start(priority=1)
    for t in range(tm):
        row_copy(t, 0).wait()
        row_copy(t, 0).wait()


def _dispatch_call(tail_start, n_active, dest0, dest1, h3, n_rows):
    t, d = h3.shape
    tm = MOVE_TILE
    nt = t // tm
    grid_spec = pltpu.PrefetchScalarGridSpec(
        num_scalar_prefetch=2,
        grid=(nt,),
        in_specs=[
            pl.BlockSpec((None, 1, tm), lambda i, *_: (i, 0, 0), memory_space=pltpu.SMEM),
            pl.BlockSpec((None, 1, tm), lambda i, *_: (i, 0, 0), memory_space=pltpu.SMEM),
            pl.BlockSpec((tm, d), lambda i, *_: (i, 0)),
        ],
        out_specs=pl.BlockSpec(memory_space=pl.ANY),
        scratch_shapes=[
            pltpu.VMEM((ROW_BLOCK, d), F32),
            pltpu.SemaphoreType.DMA(()),
            pltpu.SemaphoreType.DMA(()),
        ],
    )
    return pl.pallas_call(
        _dispatch_kernel,
        grid_spec=grid_spec,
        out_shape=jax.ShapeDtypeStruct((n_rows, d), F32),
        compiler_params=pltpu.CompilerParams(
            dimension_semantics=("arbitrary",), vmem_limit_bytes=VMEM_LIMIT,
            has_side_effects=True),
        name="dispatch",
    )(tail_start, n_active, dest0.reshape(nt, 1, tm), dest1.reshape(nt, 1, tm), h3)


def _experts_kernel(be_ref, na_ref, xs_ref, wg_ref, wu_ref, wd_ref, ys_ref, wg_s, wu_s, wd_s, prev_e):
    b = pl.program_id(0)

    @pl.when(b == 0)
    def _():
        prev_e[0] = -1

    @pl.when(b < na_ref[0])
    def _():
        e = be_ref[b]

        @pl.when(e != prev_e[0])
        def _():
            wg_s[...] = wg_ref[...].astype(BF16)
            wu_s[...] = wu_ref[...].astype(BF16)
            wd_s[...] = wd_ref[...].astype(BF16)
            prev_e[0] = e

        xb = xs_ref[...].astype(BF16)
        g = jnp.dot(xb, wg_s[...], preferred_element_type=F32)
        u = jnp.dot(xb, wu_s[...], preferred_element_type=F32)
        hmid = (g * jax.nn.sigmoid(g) * u).astype(BF16)
        ys_ref[...] = jnp.dot(hmid, wd_s[...], preferred_element_type=F32)

    @pl.when(b >= na_ref[0])
    def _():
        ys_ref[...] = jnp.zeros_like(ys_ref)


def _experts_call(block_expert, n_active, xs, w_gate, w_up, w_down):
    n_rows, d = xs.shape
    nb = n_rows // ROW_BLOCK
    de = w_gate.shape[-1]

    def blk(b, be, na):
        return jnp.minimum(b, na[0] - 1)

    grid_spec = pltpu.PrefetchScalarGridSpec(
        num_scalar_prefetch=2,
        grid=(nb,),
        in_specs=[
            pl.BlockSpec((ROW_BLOCK, d), lambda b, be, na: (blk(b, be, na), 0)),
            pl.BlockSpec((None, d, de), lambda b, be, na: (be[blk(b, be, na)], 0, 0)),
            pl.BlockSpec((None, d, de), lambda b, be, na: (be[blk(b, be, na)], 0, 0)),
            pl.BlockSpec((None, de, d), lambda b, be, na: (be[blk(b, be, na)], 0, 0)),
        ],
        out_specs=pl.BlockSpec((ROW_BLOCK, d), lambda b, be, na: (b, 0)),
        scratch_shapes=[
            pltpu.VMEM((d, de), BF16),
            pltpu.VMEM((d, de), BF16),
            pltpu.VMEM((de, d), BF16),
            pltpu.SMEM((1,), jnp.int32),
        ],
    )
    return pl.pallas_call(
        _experts_kernel,
        grid_spec=grid_spec,
        out_shape=jax.ShapeDtypeStruct((n_rows, d), F32),
        compiler_params=pltpu.CompilerParams(
            dimension_semantics=("arbitrary",), vmem_limit_bytes=VMEM_LIMIT),
        name="experts",
    )(block_expert, n_active, xs, w_gate, w_up, w_down)


def _combine_kernel(d0_ref, d1_ref, x_ref, cw_ref, fw_ref, ys_ref, o_ref, y0, y1, sem):
    tm = MOVE_TILE

    def row_copy(src_row, buf, t):
        return pltpu.make_async_copy(ys_ref.at[pl.ds(src_row, 1), :], buf.at[pl.ds(t, 1), :], sem)

    for t in range(tm):
        row_copy(d0_ref[0, t], y0, t).start(priority=0)
        row_copy(d1_ref[0, t], y1, t).start(priority=1)
    for t in range(tm):
        row_copy(0, y0, t).wait()
        row_copy(0, y1, t).wait()
    cw = cw_ref[...]
    moe = cw[:, 0:1] * y0[...] + cw[:, 1:2] * y1[...]
    o_ref[...] = _rms(x_ref[...] + moe, fw_ref[...])


def _combine_call(dest0, dest1, x2, cw_t, final_w, ys):
    t, d = x2.shape
    tm = MOVE_TILE
    nt = t // tm
    return pl.pallas_call(
        _combine_kernel,
        grid=(nt,),
        in_specs=[
            pl.BlockSpec((None, 1, tm), lambda i: (i, 0, 0), memory_space=pltpu.SMEM),
            pl.BlockSpec((None, 1, tm), lambda i: (i, 0, 0), memory_space=pltpu.SMEM),
            pl.BlockSpec((tm, d), lambda i: (i, 0)),
            pl.BlockSpec((tm, 2), lambda i: (i, 0)),
            _const_spec((1, d)),
            pl.BlockSpec(memory_space=pl.ANY),
        ],
        out_specs=pl.BlockSpec((tm, d), lambda i: (i, 0)),
        out_shape=jax.ShapeDtypeStruct((t, d), F32),
        scratch_shapes=[
            pltpu.VMEM((tm, d), F32),
            pltpu.VMEM((tm, d), F32),
            pltpu.SemaphoreType.DMA(()),
        ],
        compiler_params=pltpu.CompilerParams(
            dimension_semantics=("arbitrary",), vmem_limit_bytes=VMEM_LIMIT),
        name="combine",
    )(dest0.reshape(nt, 1, tm), dest1.reshape(nt, 1, tm), x2, cw_t, final_w, ys)


def _block_diag(w):
    n, c, d = w.shape
    eye = jnp.eye(n, dtype=w.dtype)
    return (w[:, :, None, :] * eye[:, None, :, None]).reshape(n * c, n * d)


def _rel_bias_pair(rel_bias):
    nh = rel_bias.shape[0]
    far = LEFT_CHUNKS * CHUNK - REL_CLIP
    t = jnp.concatenate(
        [rel_bias[:, REL_CLIP - (CHUNK - 1):],
         jnp.broadcast_to(rel_bias[:, -1:], (nh, far + CHUNK - 1))], axis=1)
    band = jnp.stack([t[:, qi:qi + BAND] for qi in range(CHUNK)], axis=1)[:, :, ::-1]
    neg = jnp.full((nh, CHUNK, CHUNK), NEG_INF, F32)
    return jnp.concatenate(
        [jnp.concatenate([band, neg], axis=2), jnp.concatenate([neg, band], axis=2)], axis=1)


def _routing_plan(eidx, n_blocks):
    flat_e = eidx.reshape(-1)
    onehot = (flat_e[:, None] == jnp.arange(N_EXPERTS, dtype=jnp.int32)[None, :]).astype(jnp.int32)
    csum = jnp.cumsum(onehot, axis=0)
    counts = csum[-1]
    rank = jnp.sum(onehot * csum, axis=1) - 1
    padded = (counts + ROW_BLOCK - 1) // ROW_BLOCK * ROW_BLOCK
    pad_end = jnp.cumsum(padded)
    pad_start = pad_end - padded
    dest = jnp.sum(onehot * pad_start[None, :], axis=1) + rank
    block_row = jnp.arange(n_blocks, dtype=jnp.int32) * ROW_BLOCK
    block_expert = jnp.minimum(
        jnp.sum((pad_end[None, :] <= block_row[:, None]).astype(jnp.int32), axis=1), N_EXPERTS - 1)
    n_active = (pad_end[-1] // ROW_BLOCK).astype(jnp.int32).reshape(1)
    tail_start = jnp.where(padded > 0, pad_end - ROW_BLOCK, -1).astype(jnp.int32)
    return dest.astype(jnp.int32).reshape(eidx.shape), block_expert, n_active, tail_start


def kernel(x, mem, ln1_w, w_in, conv_w, conv_b, rnn_wa, rnn_ba, rnn_wx, rnn_bx, rnn_lambda, rel_bias, gn_rnn_w, gn_att_w, w_out, ln2_w, mem_norm_w, xq_w, xk_w, xv_w, xo_w, ln3_w, router_group_w, router_group_b, router_expert_w, router_expert_b, expert_gate_w, expert_up_w, expert_down_w, final_norm_w):
    b, s, d = x.shape
    t = b * s
    depth = ln1_w.shape[0]
    row = lambda v: v.reshape(1, -1)

    kmem, vmem = None, None
    for l in range(depth):
        x = _mixer_call(
            x, row(ln1_w[l]), w_in[l].astype(BF16),
            w_in[l][:, 2 * D_RNN + D_ATT:2 * D_RNN + 2 * D_ATT].T.astype(BF16),
            conv_w[l], row(conv_b[l]),
            _block_diag(rnn_wa[l]).astype(BF16), row(rnn_ba[l]),
            _block_diag(rnn_wx[l]).astype(BF16), row(rnn_bx[l]), row(rnn_lambda[l]),
            _rel_bias_pair(rel_bias[l]), row(gn_rnn_w[l]), row(gn_att_w[l]), w_out[l].astype(BF16))

        kmem, vmem = _memkv_call(mem, row(mem_norm_w), xk_w[l].astype(BF16), xv_w[l].astype(BF16))
        pad = jnp.zeros((SUBLANES - N_GROUPS, d), F32)
        wr_t = jnp.concatenate(
            [router_group_w[l].T, pad,
             router_expert_w[l].transpose(0, 2, 1).reshape(N_EXPERTS, d)], axis=0).astype(BF16)
        br_t = jnp.concatenate(
            [router_group_b[l], jnp.zeros((SUBLANES - N_GROUPS,), F32),
             router_expert_b[l].reshape(-1)]).reshape(ROUTER_ROWS, 1)
        x2, h3, eidx, cw = _xattn_router_call(
            x, row(ln2_w[l]), xq_w[l].astype(BF16), kmem, vmem, xo_w[l].astype(BF16),
            row(ln3_w[l]), wr_t, br_t)

        n_blocks = (2 * t) // ROW_BLOCK + N_EXPERTS
        eidx_t = eidx.transpose(1, 0, 2).reshape(2, t)
        dest, block_expert, n_active, tail_start = _routing_plan(eidx_t, n_blocks)
        xs = _dispatch_call(tail_start, n_active, dest[0], dest[1], h3.reshape(t, d),
                            n_blocks * ROW_BLOCK)
        ys = _experts_call(block_expert, n_active, xs, expert_gate_w[l], expert_up_w[l],
                           expert_down_w[l])
        cw_t = cw.transpose(0, 2, 1).reshape(t, 2)
        is_last = l == depth - 1
        assert is_last, "only a single layer is supported"
        x = _combine_call(dest[0], dest[1], x2.reshape(t, d), cw_t, row(final_norm_w), ys)
        x = x.reshape(b, s, d)
    return x
```

```python
import functools

import jax
import jax.numpy as jnp
import numpy as np
from jax import lax
from jax.experimental import pallas as pl
from jax.experimental.pallas import tpu as pltpu

F32 = jnp.float32
BF16 = jnp.bfloat16

D_MODEL = 1024
D_RNN = 512
D_ATT = 512
N_RNN_BLOCKS = 8
CONV_WIDTH = 4
RG_LRU_C = 8.0
N_ATT_HEADS = 8
ATT_HEAD_DIM = 64
CHUNK = 64
LEFT_CHUNKS = 8
BAND = (LEFT_CHUNKS + 1) * CHUNK
PAIR_ROWS = 2 * CHUNK
PAIR_KEYS = BAND + CHUNK
REL_CLIP = 128
N_XHEADS = 4
XHEAD_DIM = 256
N_GROUPS = 4
EXPERTS_PER_GROUP = 8
N_EXPERTS = 32
D_EXPERT = 512
EPS = 1e-6
NEG_INF = -1e30

SUBLANES = 8
SEQ_TILE = LEFT_CHUNKS * CHUNK
TOK_TILE = 512
ROW_BLOCK = 512
MOVE_TILE = 256
ROUTER_ROWS = 40
VMEM_LIMIT = 56 * 1024 * 1024


def _rms(x, w):
    return x * lax.rsqrt(jnp.mean(x * x, axis=-1, keepdims=True) + EPS) * w


def _const_spec(shape):
    return pl.BlockSpec(shape, lambda *_: (0,) * len(shape))


def _memkv_kernel(mem_ref, w_ref, wk_ref, wv_ref, k_ref, v_ref):
    mn = _rms(mem_ref[...], w_ref[...]).astype(BF16)
    k_ref[...] = jnp.dot(mn, wk_ref[...], preferred_element_type=F32).astype(BF16)
    v_ref[...] = jnp.dot(mn, wv_ref[...], preferred_element_type=F32).astype(BF16)


def _memkv_call(mem, mem_norm_w, wk, wv):
    b, m, d = mem.shape
    return pl.pallas_call(
        _memkv_kernel,
        grid=(b,),
        in_specs=[
            pl.BlockSpec((None, m, d), lambda i: (i, 0, 0)),
            _const_spec((1, d)),
            _const_spec((d, d)),
            _const_spec((d, d)),
        ],
        out_specs=[
            pl.BlockSpec((None, m, d), lambda i: (i, 0, 0)),
            pl.BlockSpec((None, m, d), lambda i: (i, 0, 0)),
        ],
        out_shape=[jax.ShapeDtypeStruct((b, m, d), BF16)] * 2,
        compiler_params=pltpu.CompilerParams(vmem_limit_bytes=VMEM_LIMIT),
        name="mem_kv",
    )(mem, mem_norm_w, wk, wv)


def _mixer_kernel(x_ref, ln1_ref, win_ref, wkt_ref, convw_ref, convb_ref, wa_ref, ba_ref, wx_ref,
                  bx_ref, lam_ref, bias_ref, gnr_ref, gna_ref, wout_ref, o_ref,
                  xr_ext, h_state, q_s, kt_s, vbuf, yatt):
    ts = SEQ_TILE
    i = pl.program_id(1)

    @pl.when(i == 0)
    def _():
        xr_ext[pl.ds(0, SUBLANES), :] = jnp.zeros((SUBLANES, D_RNN), F32)
        h_state[...] = jnp.zeros_like(h_state)
        kt_s[:, pl.ds(0, ts)] = jnp.zeros((D_ATT, ts), BF16)
        vbuf[pl.ds(0, ts), :] = jnp.zeros((ts, D_ATT), BF16)

    x = x_ref[...]
    h = _rms(x, ln1_ref[...]).astype(BF16)

    def proj(lo):
        return jnp.dot(h, win_ref[:, lo:lo + D_RNN], preferred_element_type=F32)

    xr_ext[pl.ds(SUBLANES, ts), :] = proj(0)
    xc = convb_ref[...] + sum(
        convw_ref[j:j + 1, :] * xr_ext[pl.ds(SUBLANES - (CONV_WIDTH - 1) + j, ts), :]
        for j in range(CONV_WIDTH))
    xr_ext[pl.ds(0, SUBLANES), :] = xr_ext[pl.ds(ts, SUBLANES), :]
    xcb = xc.astype(BF16)
    r = jax.nn.sigmoid(jnp.dot(xcb, wa_ref[...], preferred_element_type=F32) + ba_ref[...])
    gate_i = jax.nn.sigmoid(jnp.dot(xcb, wx_ref[...], preferred_element_type=F32) + bx_ref[...])
    z = -lam_ref[...]
    softplus = jnp.maximum(z, 0.0) + jnp.log1p(jnp.exp(-jnp.abs(z)))
    log_a = (-RG_LRU_C * r) * softplus
    a = jnp.exp(log_a)
    t = jnp.tanh(log_a)
    u = jnp.sqrt(-2.0 * t / (1.0 - t)) * (gate_i * xc)

    ng = ts // SUBLANES
    a3 = a.reshape(ng, SUBLANES, D_RNN)
    b3 = u.reshape(ng, SUBLANES, D_RNN)
    row = lax.broadcasted_iota(jnp.int32, (ng, SUBLANES, D_RNN), 1)
    for d in (1, 2, 4):
        a_sh = pltpu.roll(a3, d, 1)
        b_sh = pltpu.roll(b3, d, 1)
        m = row >= d
        b3 = jnp.where(m, a3 * b_sh + b3, b3)
        a3 = jnp.where(m, a3 * a_sh, a3)

    hprev = h_state[...]
    hs = []
    for g in range(ng):
        hg = a3[g] * hprev + b3[g]
        hs.append(hg)
        hprev = jnp.broadcast_to(hg[SUBLANES - 1:SUBLANES, :], (SUBLANES, D_RNN))
    h_state[...] = hprev
    hseq = jnp.concatenate(hs, axis=0)
    xg = proj(D_RNN)
    gelu = 0.5 * xg * (1.0 + jnp.tanh(0.7978845608028654 * (xg + 0.044715 * (xg * xg * xg))))
    y_rnn = _rms(gelu * hseq, gnr_ref[...]).astype(BF16)

    q_s[...] = (proj(2 * D_RNN) * (ATT_HEAD_DIM ** -0.5)).astype(BF16)
    kt_s[:, pl.ds(ts, ts)] = lax.dot_general(
        wkt_ref[...], h, (((1,), (1,)), ((), ())), preferred_element_type=F32).astype(BF16)
    vbuf[pl.ds(ts, ts), :] = proj(2 * D_RNN + 2 * D_ATT).astype(BF16)
    lane = lax.broadcasted_iota(jnp.int32, (PAIR_ROWS, PAIR_ROWS), 1)
    first_head = lane < ATT_HEAD_DIM
    key_pos = lax.broadcasted_iota(jnp.int32, (PAIR_ROWS, PAIR_KEYS), 1)
    zero_q = jnp.zeros((PAIR_ROWS, PAIR_ROWS), BF16)

    for jj in range(ts // PAIR_ROWS):
        w0 = jj * PAIR_ROWS
        valid = key_pos >= jnp.where(i > 0, 0, ts - w0)
        scores = []
        for hd in range(N_ATT_HEADS):
            fl = slice((hd // 2) * PAIR_ROWS, (hd // 2 + 1) * PAIR_ROWS)
            q2 = q_s[pl.ds(w0, PAIR_ROWS), fl]
            qm = jnp.where(first_head if hd % 2 == 0 else ~first_head, q2, zero_q)
            scores.append(jnp.dot(qm, kt_s[fl, pl.ds(w0, PAIR_KEYS)], preferred_element_type=F32))
        outs = []
        for hp in range(N_ATT_HEADS // 2):
            fl = slice(hp * PAIR_ROWS, (hp + 1) * PAIR_ROWS)
            v2 = vbuf[pl.ds(w0, PAIR_KEYS), fl]
            o2 = []
            for hd in (2 * hp, 2 * hp + 1):
                s = jnp.where(valid, scores[hd] + bias_ref[hd], NEG_INF)
                p = jnp.exp(s - jnp.max(s, axis=-1, keepdims=True))
                l = jnp.sum(p, axis=-1, keepdims=True)
                o2.append(jnp.dot(p.astype(BF16), v2, preferred_element_type=F32) / l)
            outs.append(jnp.where(first_head, o2[0], o2[1]))
        yatt[pl.ds(w0, PAIR_ROWS), :] = jnp.concatenate(outs, axis=-1)

    kt_s[:, pl.ds(0, ts)] = kt_s[:, pl.ds(ts, ts)]
    vbuf[pl.ds(0, ts), :] = vbuf[pl.ds(ts, ts), :]
    y_att = _rms(yatt[...], gna_ref[...]).astype(BF16)

    mix = (jnp.dot(y_rnn, wout_ref[pl.ds(0, D_RNN), :], preferred_element_type=F32)
           + jnp.dot(y_att, wout_ref[pl.ds(D_RNN, D_ATT), :], preferred_element_type=F32))
    o_ref[...] = x + mix


def _mixer_call(x, ln1_w, w_in, wk_t, conv_w, conv_b, wa_bd, ba, wx_bd, bx, lam, bias_band, gnr, gna,
                w_out):
    b, s, d = x.shape
    ts = SEQ_TILE
    assert s % ts == 0
    dp = w_in.shape[1]
    return pl.pallas_call(
        _mixer_kernel,
        grid=(b, s // ts),
        in_specs=[
            pl.BlockSpec((None, ts, d), lambda bi, i: (bi, i, 0)),
            _const_spec((1, d)),
            _const_spec((d, dp)),
            _const_spec((D_ATT, d)),
            _const_spec((CONV_WIDTH, D_RNN)),
            _const_spec((1, D_RNN)),
            _const_spec((D_RNN, D_RNN)),
            _const_spec((1, D_RNN)),
            _const_spec((D_RNN, D_RNN)),
            _const_spec((1, D_RNN)),
            _const_spec((1, D_RNN)),
            _const_spec((N_ATT_HEADS, PAIR_ROWS, PAIR_KEYS)),
            _const_spec((1, D_RNN)),
            _const_spec((1, D_ATT)),
            _const_spec((d, d)),
        ],
        out_specs=pl.BlockSpec((None, ts, d), lambda bi, i: (bi, i, 0)),
        out_shape=jax.ShapeDtypeStruct((b, s, d), F32),
        scratch_shapes=[
            pltpu.VMEM((ts + SUBLANES, D_RNN), F32),
            pltpu.VMEM((SUBLANES, D_RNN), F32),
            pltpu.VMEM((ts, D_ATT), BF16),
            pltpu.VMEM((D_ATT, 2 * ts), BF16),
            pltpu.VMEM((2 * ts, D_ATT), BF16),
            pltpu.VMEM((ts, D_ATT), F32),
        ],
        compiler_params=pltpu.CompilerParams(
            dimension_semantics=("arbitrary", "arbitrary"), vmem_limit_bytes=VMEM_LIMIT),
        name="mixer",
    )(x, ln1_w, w_in, wk_t, conv_w, conv_b, wa_bd, ba, wx_bd, bx, lam, bias_band, gnr, gna, w_out)


def _xattn_router_kernel(x_ref, ln2_ref, wq_ref, k_ref, v_ref, wo_ref, ln3_ref, wr_ref, br_ref,
                         x2_ref, h3_ref, eidx_ref, cw_ref):
    x = x_ref[...]
    h = _rms(x, ln2_ref[...]).astype(BF16)
    q = (jnp.dot(h, wq_ref[...], preferred_element_type=F32) * (XHEAD_DIM ** -0.5)).astype(BF16)
    outs = []
    for hd in range(N_XHEADS):
        sl = slice(hd * XHEAD_DIM, (hd + 1) * XHEAD_DIM)
        s = lax.dot_general(q[:, sl], k_ref[:, sl], (((1,), (1,)), ((), ())),
                            preferred_element_type=F32)
        p = jnp.exp(s - jnp.max(s, axis=-1, keepdims=True))
        l = jnp.sum(p, axis=-1, keepdims=True)
        o = jnp.dot(p.astype(BF16), v_ref[:, sl], preferred_element_type=F32)
        outs.append((o / l).astype(BF16))
    att = jnp.concatenate(outs, axis=-1)
    x2 = x + jnp.dot(att, wo_ref[...], preferred_element_type=F32)
    x2_ref[...] = x2

    h3 = _rms(x2, ln3_ref[...])
    h3_ref[...] = h3
    logits = lax.dot_general(wr_ref[...], h3.astype(BF16), (((1,), (1,)), ((), ())),
                             preferred_element_type=F32) + br_ref[...]
    gl = logits[0:N_GROUPS, :]
    ge = jnp.exp(gl - jnp.max(gl, axis=0, keepdims=True))
    gp = ge / jnp.sum(ge, axis=0, keepdims=True)
    g_p = jnp.max(gp, axis=0, keepdims=True)
    g_iota = lax.broadcasted_iota(jnp.int32, gp.shape, 0)
    g_idx = jnp.min(jnp.where(gp == g_p, g_iota, N_GROUPS), axis=0, keepdims=True)

    el = jnp.zeros((EXPERTS_PER_GROUP, logits.shape[1]), F32)
    for g in range(N_GROUPS):
        lo = SUBLANES + g * EXPERTS_PER_GROUP
        el = jnp.where(g_idx == g, logits[lo:lo + EXPERTS_PER_GROUP, :], el)
    ee = jnp.exp(el - jnp.max(el, axis=0, keepdims=True))
    ep = ee / jnp.sum(ee, axis=0, keepdims=True)
    e_iota = lax.broadcasted_iota(jnp.int32, ep.shape, 0)
    p1 = jnp.max(ep, axis=0, keepdims=True)
    i1 = jnp.min(jnp.where(ep == p1, e_iota, EXPERTS_PER_GROUP), axis=0, keepdims=True)
    ep2 = jnp.where(e_iota == i1, -1.0, ep)
    p2 = jnp.max(ep2, axis=0, keepdims=True)
    i2 = jnp.min(jnp.where(ep2 == p2, e_iota, EXPERTS_PER_GROUP), axis=0, keepdims=True)
    den = p1 + p2
    eidx_ref[...] = jnp.concatenate(
        [g_idx * EXPERTS_PER_GROUP + i1, g_idx * EXPERTS_PER_GROUP + i2], axis=0)
    cw_ref[...] = jnp.concatenate([g_p * (p1 / den), g_p * (p2 / den)], axis=0)


def _xattn_router_call(x1, ln2_w, wq, kmem, vmem, wo, ln3_w, wr_t, br_t):
    b, s, d = x1.shape
    tt = TOK_TILE
    m = kmem.shape[1]
    return pl.pallas_call(
        _xattn_router_kernel,
        grid=(b, s // tt),
        in_specs=[
            pl.BlockSpec((None, tt, d), lambda bi, i: (bi, i, 0)),
            _const_spec((1, d)),
            _const_spec((d, d)),
            pl.BlockSpec((None, m, d), lambda bi, i: (bi, 0, 0)),
            pl.BlockSpec((None, m, d), lambda bi, i: (bi, 0, 0)),
            _const_spec((d, d)),
            _const_spec((1, d)),
            _const_spec((ROUTER_ROWS, d)),
            _const_spec((ROUTER_ROWS, 1)),
        ],
        out_specs=[
            pl.BlockSpec((None, tt, d), lambda bi, i: (bi, i, 0)),
            pl.BlockSpec((None, tt, d), lambda bi, i: (bi, i, 0)),
            pl.BlockSpec((None, 2, tt), lambda bi, i: (bi, 0, i)),
            pl.BlockSpec((None, 2, tt), lambda bi, i: (bi, 0, i)),
        ],
        out_shape=[
            jax.ShapeDtypeStruct((b, s, d), F32),
            jax.ShapeDtypeStruct((b, s, d), F32),
            jax.ShapeDtypeStruct((b, 2, s), jnp.int32),
            jax.ShapeDtypeStruct((b, 2, s), F32),
        ],
        compiler_params=pltpu.CompilerParams(
            dimension_semantics=("arbitrary", "arbitrary"), vmem_limit_bytes=VMEM_LIMIT),
        name="xattn_router",
    )(x1, ln2_w, wq, kmem, vmem, wo, ln3_w, wr_t, br_t)


def _dispatch_kernel(tail_ref, na_ref, d0_ref, d1_ref, h_ref, xs_ref, zbuf, sem, zsem):
    i = pl.program_id(0)
    tm = MOVE_TILE
    nb = xs_ref.shape[0] // ROW_BLOCK

    @pl.when(i == 0)
    def _():
        zbuf[...] = jnp.zeros_like(zbuf)

        def zero_copy(start_row):
            start_row = pl.multiple_of(start_row, ROW_BLOCK)
            return pltpu.make_async_copy(zbuf, xs_ref.at[pl.ds(start_row, ROW_BLOCK), :], zsem)

        def start(e, c):
            @pl.when(tail_ref[e] >= 0)
            def _():
                zero_copy(tail_ref[e]).start()
            return c

        def wait(e, c):
            @pl.when(tail_ref[e] >= 0)
            def _():
                zero_copy(tail_ref[e]).wait()
            return c

        def start_unused(b, c):
            zero_copy(b * ROW_BLOCK).start()
            return c

        def wait_unused(b, c):
            zero_copy(b * ROW_BLOCK).wait()
            return c

        lax.fori_loop(0, N_EXPERTS, start, 0)
        lax.fori_loop(na_ref[0], nb, start_unused, 0)
        lax.fori_loop(0, N_EXPERTS, wait, 0)
        lax.fori_loop(na_ref[0], nb, wait_unused, 0)

    def row_copy(t, dst_row):
        return pltpu.make_async_copy(h_ref.at[pl.ds(t, 1), :], xs_ref.at[pl.ds(dst_row, 1), :], sem)

    for t in range(tm):
        row_copy(t, d0_ref[0, t]).start(priority=0)
        row_copy(t, d1_ref[0, t]).start(priority=1)
    for t in range(tm):
        row_copy(t, 0).wait()
        row_copy(t, 0).wait()


def _dispatch_call(tail_start, n_active, dest0, dest1, h3, n_rows):
    t, d = h3.shape
    tm = MOVE_TILE
    nt = t // tm
    grid_spec = pltpu.PrefetchScalarGridSpec(
        num_scalar_prefetch=2,
        grid=(nt,),
        in_specs=[
            pl.BlockSpec((None, 1, tm), lambda i, *_: (i, 0, 0), memory_space=pltpu.SMEM),
            pl.BlockSpec((None, 1, tm), lambda i, *_: (i, 0, 0), memory_space=pltpu.SMEM),
            pl.BlockSpec((tm, d), lambda i, *_: (i, 0)),
        ],
        out_specs=pl.BlockSpec(memory_space=pl.ANY),
        scratch_shapes=[
            pltpu.VMEM((ROW_BLOCK, d), F32),
            pltpu.SemaphoreType.DMA(()),
            pltpu.SemaphoreType.DMA(()),
        ],
    )
    return pl.pallas_call(
        _dispatch_kernel,
        grid_spec=grid_spec,
        out_shape=jax.ShapeDtypeStruct((n_rows, d), F32),
        compiler_params=pltpu.CompilerParams(
            dimension_semantics=("arbitrary",), vmem_limit_bytes=VMEM_LIMIT,
            has_side_effects=True),
        name="dispatch",
    )(tail_start, n_active, dest0.reshape(nt, 1, tm), dest1.reshape(nt, 1, tm), h3)


def _experts_kernel(be_ref, na_ref, tgt_ref, xs_ref, wg_ref, wu_ref, wd_ref, out_ref,
                    ybuf, wg_s, wu_s, wd_s, prev_e, sems):
    s = pl.program_id(0)
    na = na_ref[0]
    cur = s % 2

    @pl.when(s == 0)
    def _():
        prev_e[0] = -1

    def row_copy(buf, k, dst_row):
        return pltpu.make_async_copy(
            ybuf.at[buf, pl.ds(k, 1), :], out_ref.at[pl.ds(dst_row, 1), :], sems.at[buf])

    def drain(buf):
        for k in range(ROW_BLOCK):
            row_copy(buf, k, 0).wait()

    def scatter(buf):
        for k in range(ROW_BLOCK):
            row_copy(buf, k, tgt_ref[0, k]).start(priority=k % 2)

    def load_weights():
        e = be_ref[s]

        @pl.when(e != prev_e[0])
        def _():
            wg_s[...] = wg_ref[...].astype(BF16)
            wu_s[...] = wu_ref[...].astype(BF16)
            wd_s[...] = wd_ref[...].astype(BF16)
            prev_e[0] = e

    def compute(buf):
        half = ROW_BLOCK // 2
        for hh in range(2):
            xb = xs_ref[pl.ds(hh * half, half), :].astype(BF16)
            g = jnp.dot(xb, wg_s[...], preferred_element_type=F32)
            u = jnp.dot(xb, wu_s[...], preferred_element_type=F32)
            hmid = (g * jax.nn.sigmoid(g) * u).astype(BF16)
            ybuf[buf, pl.ds(hh * half, half), :] = jnp.dot(
                hmid, wd_s[...], preferred_element_type=F32)

    @pl.when(jnp.logical_and(s >= 2, s - 2 < na))
    def _():
        drain(cur)

    @pl.when(s < na)
    def _():
        load_weights()

    @pl.when(s == 0)
    def _():
        n_slots = out_ref.shape[0] - 2 * ROW_BLOCK
        ybuf[1] = jnp.zeros((ROW_BLOCK, ybuf.shape[2]), F32)
        spare = [pltpu.make_async_copy(
            ybuf.at[1], out_ref.at[pl.ds(n_slots + p * ROW_BLOCK, ROW_BLOCK), :], sems.at[1])
            for p in range(2)]
        for cp in spare:
            cp.start()
        for cp in spare:
            cp.wait()
        compute(cur)

    @pl.when(jnp.logical_and(s >= 1, s < na))
    def _():
        scatter(1 - cur)
        compute(cur)

    @pl.when(s == na)
    def _():
        scatter(1 - cur)


def _experts_call(block_expert, n_active, row_target, xs, w_gate, w_up, w_down, n_out_rows):
    n_rows, d = xs.shape
    nb = n_rows // ROW_BLOCK
    de = w_gate.shape[-1]

    def blk(s, be, na):
        return jnp.minimum(s, na[0] - 1)

    def prev_blk(s, be, na):
        return jnp.clip(s - 1, 0, na[0] - 1)

    grid_spec = pltpu.PrefetchScalarGridSpec(
        num_scalar_prefetch=2,
        grid=(nb + 2,),
        in_specs=[
            pl.BlockSpec((None, 1, ROW_BLOCK), lambda s, be, na: (prev_blk(s, be, na), 0, 0),
                         memory_space=pltpu.SMEM),
            pl.BlockSpec((ROW_BLOCK, d), lambda s, be, na: (blk(s, be, na), 0)),
            pl.BlockSpec((None, d, de), lambda s, be, na: (be[blk(s, be, na)], 0, 0)),
            pl.BlockSpec((None, d, de), lambda s, be, na: (be[blk(s, be, na)], 0, 0)),
            pl.BlockSpec((None, de, d), lambda s, be, na: (be[blk(s, be, na)], 0, 0)),
        ],
        out_specs=pl.BlockSpec(memory_space=pl.ANY),
        scratch_shapes=[
            pltpu.VMEM((2, ROW_BLOCK, d), F32),
            pltpu.VMEM((d, de), BF16),
            pltpu.VMEM((d, de), BF16),
            pltpu.VMEM((de, d), BF16),
            pltpu.SMEM((1,), jnp.int32),
            pltpu.SemaphoreType.DMA((2,)),
        ],
    )
    return pl.pallas_call(
        _experts_kernel,
        grid_spec=grid_spec,
        out_shape=jax.ShapeDtypeStruct((n_out_rows, d), F32),
        compiler_params=pltpu.CompilerParams(
            dimension_semantics=("arbitrary",), vmem_limit_bytes=VMEM_LIMIT,
            has_side_effects=True),
        name="experts",
    )(block_expert, n_active, row_target.reshape(nb, 1, ROW_BLOCK), xs, w_gate, w_up, w_down)


def _combine_kernel(x_ref, y0_ref, y1_ref, cw_ref, fw_ref, o_ref):
    cw = cw_ref[...]
    moe = cw[:, 0:1] * y0_ref[...] + cw[:, 1:2] * y1_ref[...]
    o_ref[...] = _rms(x_ref[...] + moe, fw_ref[...])


def _combine_call(x2, y_slots, cw_t, final_w):
    t, d = x2.shape
    tm = TOK_TILE
    nt = t // tm
    return pl.pallas_call(
        _combine_kernel,
        grid=(nt,),
        in_specs=[
            pl.BlockSpec((tm, d), lambda i: (i, 0)),
            pl.BlockSpec((tm, d), lambda i: (i, 0)),
            pl.BlockSpec((tm, d), lambda i: (nt + i, 0)),
            pl.BlockSpec((tm, 2), lambda i: (i, 0)),
            _const_spec((1, d)),
        ],
        out_specs=pl.BlockSpec((tm, d), lambda i: (i, 0)),
        out_shape=jax.ShapeDtypeStruct((t, d), F32),
        compiler_params=pltpu.CompilerParams(
            dimension_semantics=("arbitrary",), vmem_limit_bytes=VMEM_LIMIT),
        name="combine",
    )(x2, y_slots, y_slots, cw_t, final_w)


def _block_diag(w):
    n, c, d = w.shape
    eye = jnp.eye(n, dtype=w.dtype)
    return (w[:, :, None, :] * eye[:, None, :, None]).reshape(n * c, n * d)


def _rel_bias_pair(rel_bias):
    nh = rel_bias.shape[0]
    far = LEFT_CHUNKS * CHUNK - REL_CLIP
    t = jnp.concatenate(
        [rel_bias[:, REL_CLIP - (CHUNK - 1):],
         jnp.broadcast_to(rel_bias[:, -1:], (nh, far + CHUNK - 1))], axis=1)
    band = jnp.stack([t[:, qi:qi + BAND] for qi in range(CHUNK)], axis=1)[:, :, ::-1]
    neg = jnp.full((nh, CHUNK, CHUNK), NEG_INF, F32)
    return jnp.concatenate(
        [jnp.concatenate([band, neg], axis=2), jnp.concatenate([neg, band], axis=2)], axis=1)


def _routing_plan(eidx, n_blocks):
    flat_e = eidx.reshape(-1)
    onehot = (flat_e[:, None] == jnp.arange(N_EXPERTS, dtype=jnp.int32)[None, :]).astype(jnp.int32)
    csum = jnp.cumsum(onehot, axis=0)
    counts = csum[-1]
    rank = jnp.sum(onehot * csum, axis=1) - 1
    padded = (counts + ROW_BLOCK - 1) // ROW_BLOCK * ROW_BLOCK
    pad_end = jnp.cumsum(padded)
    pad_start = pad_end - padded
    dest = jnp.sum(onehot * pad_start[None, :], axis=1) + rank
    block_row = jnp.arange(n_blocks, dtype=jnp.int32) * ROW_BLOCK
    block_expert = jnp.minimum(
        jnp.sum((pad_end[None, :] <= block_row[:, None]).astype(jnp.int32), axis=1), N_EXPERTS - 1)
    n_active = (pad_end[-1] // ROW_BLOCK).astype(jnp.int32).reshape(1)
    tail_start = jnp.where(padded > 0, pad_end - ROW_BLOCK, -1).astype(jnp.int32)
    dest = dest.astype(jnp.int32)
    n_slots = flat_e.shape[0]
    n_rows = n_blocks * ROW_BLOCK
    row_slot = jnp.full((n_rows,), -1, jnp.int32).at[dest].set(
        jnp.arange(n_slots, dtype=jnp.int32), unique_indices=True)
    rows = jnp.arange(n_rows, dtype=jnp.int32)
    spare = n_slots + (rows // ROW_BLOCK % 2) * ROW_BLOCK + rows % ROW_BLOCK
    row_target = jnp.where(row_slot >= 0, row_slot, spare)
    return dest.reshape(eidx.shape), block_expert, n_active, tail_start, row_target


def kernel(x, mem, ln1_w, w_in, conv_w, conv_b, rnn_wa, rnn_ba, rnn_wx, rnn_bx, rnn_lambda, rel_bias, gn_rnn_w, gn_att_w, w_out, ln2_w, mem_norm_w, xq_w, xk_w, xv_w, xo_w, ln3_w, router_group_w, router_group_b, router_expert_w, router_expert_b, expert_gate_w, expert_up_w, expert_down_w, final_norm_w):
    b, s, d = x.shape
    t = b * s
    depth = ln1_w.shape[0]
    row = lambda v: v.reshape(1, -1)

    kmem, vmem = None, None
    for l in range(depth):
        x = _mixer_call(
            x, row(ln1_w[l]), w_in[l].astype(BF16),
            w_in[l][:, 2 * D_RNN + D_ATT:2 * D_RNN + 2 * D_ATT].T.astype(BF16),
            conv_w[l], row(conv_b[l]),
            _block_diag(rnn_wa[l]).astype(BF16), row(rnn_ba[l]),
            _block_diag(rnn_wx[l]).astype(BF16), row(rnn_bx[l]), row(rnn_lambda[l]),
            _rel_bias_pair(rel_bias[l]), row(gn_rnn_w[l]), row(gn_att_w[l]), w_out[l].astype(BF16))

        kmem, vmem = _memkv_call(mem, row(mem_norm_w), xk_w[l].astype(BF16), xv_w[l].astype(BF16))
        pad = jnp.zeros((SUBLANES - N_GROUPS, d), F32)
        wr_t = jnp.concatenate(
            [router_group_w[l].T, pad,
             router_expert_w[l].transpose(0, 2, 1).reshape(N_EXPERTS, d)], axis=0).astype(BF16)
        br_t = jnp.concatenate(
            [router_group_b[l], jnp.zeros((SUBLANES - N_GROUPS,), F32),
             router_expert_b[l].reshape(-1)]).reshape(ROUTER_ROWS, 1)
        x2, h3, eidx, cw = _xattn_router_call(
            x, row(ln2_w[l]), xq_w[l].astype(BF16), kmem, vmem, xo_w[l].astype(BF16),
            row(ln3_w[l]), wr_t, br_t)

        n_blocks = (2 * t) // ROW_BLOCK + N_EXPERTS
        eidx_t = eidx.transpose(1, 0, 2).reshape(2, t)
        dest, block_expert, n_active, tail_start, row_target = _routing_plan(eidx_t, n_blocks)
        xs = _dispatch_call(tail_start, n_active, dest[0], dest[1], h3.reshape(t, d),
                            n_blocks * ROW_BLOCK)
        y_slots = _experts_call(block_expert, n_active, row_target, xs, expert_gate_w[l],
                                expert_up_w[l], expert_down_w[l], 2 * t + 2 * ROW_BLOCK)
        cw_t = cw.transpose(0, 2, 1).reshape(t, 2)
        is_last = l == depth - 1
        assert is_last, "only a single layer is supported"
        x = _combine_call(x2.reshape(t, d), y_slots, cw_t, row(final_norm_w))
        x = x.reshape(b, s, d)
    return x
```

```python
import functools

import jax
import jax.numpy as jnp
import numpy as np
from jax import lax
from jax.experimental import pallas as pl
from jax.experimental.pallas import tpu as pltpu

F32 = jnp.float32
BF16 = jnp.bfloat16

D_MODEL = 1024
D_RNN = 512
D_ATT = 512
N_RNN_BLOCKS = 8
CONV_WIDTH = 4
RG_LRU_C = 8.0
N_ATT_HEADS = 8
ATT_HEAD_DIM = 64
CHUNK = 64
LEFT_CHUNKS = 8
BAND = (LEFT_CHUNKS + 1) * CHUNK
PAIR_ROWS = 2 * CHUNK
PAIR_KEYS = BAND + CHUNK
REL_CLIP = 128
N_XHEADS = 4
XHEAD_DIM = 256
N_GROUPS = 4
EXPERTS_PER_GROUP = 8
N_EXPERTS = 32
D_EXPERT = 512
EPS = 1e-6
NEG_INF = -1e30

SUBLANES = 8
SEQ_TILE = LEFT_CHUNKS * CHUNK
TOK_TILE = 512
ROW_BLOCK = 512
MOVE_TILE = 256
ROUTER_ROWS = 40
VMEM_LIMIT = 56 * 1024 * 1024


def _rms(x, w):
    return x * lax.rsqrt(jnp.mean(x * x, axis=-1, keepdims=True) + EPS) * w


def _const_spec(shape):
    return pl.BlockSpec(shape, lambda *_: (0,) * len(shape))


def _memkv_kernel(mem_ref, w_ref, wk_ref, wv_ref, k_ref, v_ref):
    mn = _rms(mem_ref[...], w_ref[...]).astype(BF16)
    k_ref[...] = jnp.dot(mn, wk_ref[...], preferred_element_type=F32).astype(BF16)
    v_ref[...] = jnp.dot(mn, wv_ref[...], preferred_element_type=F32).astype(BF16)


def _memkv_call(mem, mem_norm_w, wk, wv):
    b, m, d = mem.shape
    return pl.pallas_call(
        _memkv_kernel,
        grid=(b,),
        in_specs=[
            pl.BlockSpec((None, m, d), lambda i: (i, 0, 0)),
            _const_spec((1, d)),
            _const_spec((d, d)),
            _const_spec((d, d)),
        ],
        out_specs=[
            pl.BlockSpec((None, m, d), lambda i: (i, 0, 0)),
            pl.BlockSpec((None, m, d), lambda i: (i, 0, 0)),
        ],
        out_shape=[jax.ShapeDtypeStruct((b, m, d), BF16)] * 2,
        compiler_params=pltpu.CompilerParams(vmem_limit_bytes=VMEM_LIMIT),
        name="mem_kv",
    )(mem, mem_norm_w, wk, wv)


def _mixer_kernel(x_ref, ln1_ref, win_ref, wkt_ref, convw_ref, convb_ref, wa_ref, ba_ref, wx_ref,
                  bx_ref, lam_ref, bias_ref, gnr_ref, gna_ref, wout_ref, o_ref,
                  xr_ext, h_state, q_s, kt_s, vbuf, yatt):
    ts = SEQ_TILE
    i = pl.program_id(1)

    @pl.when(i == 0)
    def _():
        xr_ext[pl.ds(0, SUBLANES), :] = jnp.zeros((SUBLANES, D_RNN), F32)
        h_state[...] = jnp.zeros_like(h_state)
        kt_s[:, pl.ds(0, ts)] = jnp.zeros((D_ATT, ts), BF16)
        vbuf[pl.ds(0, ts), :] = jnp.zeros((ts, D_ATT), BF16)

    x = x_ref[...]
    h = _rms(x, ln1_ref[...]).astype(BF16)

    def proj(lo):
        return jnp.dot(h, win_ref[:, lo:lo + D_RNN], preferred_element_type=F32)

    xr_ext[pl.ds(SUBLANES, ts), :] = proj(0)
    xc = convb_ref[...] + sum(
        convw_ref[j:j + 1, :] * xr_ext[pl.ds(SUBLANES - (CONV_WIDTH - 1) + j, ts), :]
        for j in range(CONV_WIDTH))
    xr_ext[pl.ds(0, SUBLANES), :] = xr_ext[pl.ds(ts, SUBLANES), :]
    xcb = xc.astype(BF16)
    r = jax.nn.sigmoid(jnp.dot(xcb, wa_ref[...], preferred_element_type=F32) + ba_ref[...])
    gate_i = jax.nn.sigmoid(jnp.dot(xcb, wx_ref[...], preferred_element_type=F32) + bx_ref[...])
    z = -lam_ref[...]
    softplus = jnp.maximum(z, 0.0) + jnp.log1p(jnp.exp(-jnp.abs(z)))
    log_a = (-RG_LRU_C * r) * softplus
    a = jnp.exp(log_a)
    t = jnp.tanh(log_a)
    u = jnp.sqrt(-2.0 * t / (1.0 - t)) * (gate_i * xc)

    ng = ts // SUBLANES
    a3 = a.reshape(ng, SUBLANES, D_RNN)
    b3 = u.reshape(ng, SUBLANES, D_RNN)
    row = lax.broadcasted_iota(jnp.int32, (ng, SUBLANES, D_RNN), 1)
    for d in (1, 2, 4):
        a_sh = pltpu.roll(a3, d, 1)
        b_sh = pltpu.roll(b3, d, 1)
        m = row >= d
        b3 = jnp.where(m, a3 * b_sh + b3, b3)
        a3 = jnp.where(m, a3 * a_sh, a3)

    hprev = h_state[...]
    hs = []
    for g in range(ng):
        hg = a3[g] * hprev + b3[g]
        hs.append(hg)
        hprev = jnp.broadcast_to(hg[SUBLANES - 1:SUBLANES, :], (SUBLANES, D_RNN))
    h_state[...] = hprev
    hseq = jnp.concatenate(hs, axis=0)
    xg = proj(D_RNN)
    gelu = 0.5 * xg * (1.0 + jnp.tanh(0.7978845608028654 * (xg + 0.044715 * (xg * xg * xg))))
    y_rnn = _rms(gelu * hseq, gnr_ref[...]).astype(BF16)

    q_s[...] = (proj(2 * D_RNN) * (ATT_HEAD_DIM ** -0.5)).astype(BF16)
    kt_s[:, pl.ds(ts, ts)] = lax.dot_general(
        wkt_ref[...], h, (((1,), (1,)), ((), ())), preferred_element_type=F32).astype(BF16)
    vbuf[pl.ds(ts, ts), :] = proj(2 * D_RNN + 2 * D_ATT).astype(BF16)
    lane = lax.broadcasted_iota(jnp.int32, (PAIR_ROWS, PAIR_ROWS), 1)
    first_head = lane < ATT_HEAD_DIM
    key_pos = lax.broadcasted_iota(jnp.int32, (PAIR_ROWS, PAIR_KEYS), 1)
    zero_q = jnp.zeros((PAIR_ROWS, PAIR_ROWS), BF16)

    for jj in range(ts // PAIR_ROWS):
        w0 = jj * PAIR_ROWS
        valid = key_pos >= jnp.where(i > 0, 0, ts - w0)
        scores = []
        for hd in range(N_ATT_HEADS):
            fl = slice((hd // 2) * PAIR_ROWS, (hd // 2 + 1) * PAIR_ROWS)
            q2 = q_s[pl.ds(w0, PAIR_ROWS), fl]
            qm = jnp.where(first_head if hd % 2 == 0 else ~first_head, q2, zero_q)
            scores.append(jnp.dot(qm, kt_s[fl, pl.ds(w0, PAIR_KEYS)], preferred_element_type=F32))
        outs = []
        for hp in range(N_ATT_HEADS // 2):
            fl = slice(hp * PAIR_ROWS, (hp + 1) * PAIR_ROWS)
            v2 = vbuf[pl.ds(w0, PAIR_KEYS), fl]
            o2 = []
            for hd in (2 * hp, 2 * hp + 1):
                s = jnp.where(valid, scores[hd] + bias_ref[hd], NEG_INF)
                p = jnp.exp(s - jnp.max(s, axis=-1, keepdims=True))
                l = jnp.sum(p, axis=-1, keepdims=True)
                o2.append(jnp.dot(p.astype(BF16), v2, preferred_element_type=F32) / l)
            outs.append(jnp.where(first_head, o2[0], o2[1]))
        yatt[pl.ds(w0, PAIR_ROWS), :] = jnp.concatenate(outs, axis=-1)

    kt_s[:, pl.ds(0, ts)] = kt_s[:, pl.ds(ts, ts)]
    vbuf[pl.ds(0, ts), :] = vbuf[pl.ds(ts, ts), :]
    y_att = _rms(yatt[...], gna_ref[...]).astype(BF16)

    mix = (jnp.dot(y_rnn, wout_ref[pl.ds(0, D_RNN), :], preferred_element_type=F32)
           + jnp.dot(y_att, wout_ref[pl.ds(D_RNN, D_ATT), :], preferred_element_type=F32))
    o_ref[...] = x + mix


def _mixer_call(x, ln1_w, w_in, wk_t, conv_w, conv_b, wa_bd, ba, wx_bd, bx, lam, bias_band, gnr, gna,
                w_out):
    b, s, d = x.shape
    ts = SEQ_TILE
    assert s % ts == 0
    dp = w_in.shape[1]
    return pl.pallas_call(
        _mixer_kernel,
        grid=(b, s // ts),
        in_specs=[
            pl.BlockSpec((None, ts, d), lambda bi, i: (bi, i, 0)),
            _const_spec((1, d)),
            _const_spec((d, dp)),
            _const_spec((D_ATT, d)),
            _const_spec((CONV_WIDTH, D_RNN)),
            _const_spec((1, D_RNN)),
            _const_spec((D_RNN, D_RNN)),
            _const_spec((1, D_RNN)),
            _const_spec((D_RNN, D_RNN)),
            _const_spec((1, D_RNN)),
            _const_spec((1, D_RNN)),
            _const_spec((N_ATT_HEADS, PAIR_ROWS, PAIR_KEYS)),
            _const_spec((1, D_RNN)),
            _const_spec((1, D_ATT)),
            _const_spec((d, d)),
        ],
        out_specs=pl.BlockSpec((None, ts, d), lambda bi, i: (bi, i, 0)),
        out_shape=jax.ShapeDtypeStruct((b, s, d), F32),
        scratch_shapes=[
            pltpu.VMEM((ts + SUBLANES, D_RNN), F32),
            pltpu.VMEM((SUBLANES, D_RNN), F32),
            pltpu.VMEM((ts, D_ATT), BF16),
            pltpu.VMEM((D_ATT, 2 * ts), BF16),
            pltpu.VMEM((2 * ts, D_ATT), BF16),
            pltpu.VMEM((ts, D_ATT), F32),
        ],
        compiler_params=pltpu.CompilerParams(
            dimension_semantics=("arbitrary", "arbitrary"), vmem_limit_bytes=VMEM_LIMIT),
        name="mixer",
    )(x, ln1_w, w_in, wk_t, conv_w, conv_b, wa_bd, ba, wx_bd, bx, lam, bias_band, gnr, gna, w_out)


def _xattn_router_kernel(x_ref, ln2_ref, wq_ref, k_ref, v_ref, wo_ref, ln3_ref, wr_ref, br_ref,
                         x2_ref, h3_ref, eidx_ref, cw_ref):
    x = x_ref[...]
    h = _rms(x, ln2_ref[...]).astype(BF16)
    q = (jnp.dot(h, wq_ref[...], preferred_element_type=F32) * (XHEAD_DIM ** -0.5)).astype(BF16)
    outs = []
    for hd in range(N_XHEADS):
        sl = slice(hd * XHEAD_DIM, (hd + 1) * XHEAD_DIM)
        s = lax.dot_general(q[:, sl], k_ref[:, sl], (((1,), (1,)), ((), ())),
                            preferred_element_type=F32)
        p = jnp.exp(s - jnp.max(s, axis=-1, keepdims=True))
        l = jnp.sum(p, axis=-1, keepdims=True)
        o = jnp.dot(p.astype(BF16), v_ref[:, sl], preferred_element_type=F32)
        outs.append((o / l).astype(BF16))
    att = jnp.concatenate(outs, axis=-1)
    x2 = x + jnp.dot(att, wo_ref[...], preferred_element_type=F32)
    x2_ref[...] = x2

    h3 = _rms(x2, ln3_ref[...])
    h3_ref[...] = h3
    logits = lax.dot_general(wr_ref[...], h3.astype(BF16), (((1,), (1,)), ((), ())),
                             preferred_element_type=F32) + br_ref[...]
    gl = logits[0:N_GROUPS, :]
    ge = jnp.exp(gl - jnp.max(gl, axis=0, keepdims=True))
    gp = ge / jnp.sum(ge, axis=0, keepdims=True)
    g_p = jnp.max(gp, axis=0, keepdims=True)
    g_iota = lax.broadcasted_iota(jnp.int32, gp.shape, 0)
    g_idx = jnp.min(jnp.where(gp == g_p, g_iota, N_GROUPS), axis=0, keepdims=True)

    el = jnp.zeros((EXPERTS_PER_GROUP, logits.shape[1]), F32)
    for g in range(N_GROUPS):
        lo = SUBLANES + g * EXPERTS_PER_GROUP
        el = jnp.where(g_idx == g, logits[lo:lo + EXPERTS_PER_GROUP, :], el)
    ee = jnp.exp(el - jnp.max(el, axis=0, keepdims=True))
    ep = ee / jnp.sum(ee, axis=0, keepdims=True)
    e_iota = lax.broadcasted_iota(jnp.int32, ep.shape, 0)
    p1 = jnp.max(ep, axis=0, keepdims=True)
    i1 = jnp.min(jnp.where(ep == p1, e_iota, EXPERTS_PER_GROUP), axis=0, keepdims=True)
    ep2 = jnp.where(e_iota == i1, -1.0, ep)
    p2 = jnp.max(ep2, axis=0, keepdims=True)
    i2 = jnp.min(jnp.where(ep2 == p2, e_iota, EXPERTS_PER_GROUP), axis=0, keepdims=True)
    den = p1 + p2
    eidx_ref[...] = jnp.concatenate(
        [g_idx * EXPERTS_PER_GROUP + i1, g_idx * EXPERTS_PER_GROUP + i2], axis=0)
    cw_ref[...] = jnp.concatenate([g_p * (p1 / den), g_p * (p2 / den)], axis=0)


def _xattn_router_call(x1, ln2_w, wq, kmem, vmem, wo, ln3_w, wr_t, br_t):
    b, s, d = x1.shape
    tt = TOK_TILE
    m = kmem.shape[1]
    return pl.pallas_call(
        _xattn_router_kernel,
        grid=(b, s // tt),
        in_specs=[
            pl.BlockSpec((None, tt, d), lambda bi, i: (bi, i, 0)),
            _const_spec((1, d)),
            _const_spec((d, d)),
            pl.BlockSpec((None, m, d), lambda bi, i: (bi, 0, 0)),
            pl.BlockSpec((None, m, d), lambda bi, i: (bi, 0, 0)),
            _const_spec((d, d)),
            _const_spec((1, d)),
            _const_spec((ROUTER_ROWS, d)),
            _const_spec((ROUTER_ROWS, 1)),
        ],
        out_specs=[
            pl.BlockSpec((None, tt, d), lambda bi, i: (bi, i, 0)),
            pl.BlockSpec((None, tt, d), lambda bi, i: (bi, i, 0)),
            pl.BlockSpec((None, 2, tt), lambda bi, i: (bi, 0, i)),
            pl.BlockSpec((None, 2, tt), lambda bi, i: (bi, 0, i)),
        ],
        out_shape=[
            jax.ShapeDtypeStruct((b, s, d), F32),
            jax.ShapeDtypeStruct((b, s, d), F32),
            jax.ShapeDtypeStruct((b, 2, s), jnp.int32),
            jax.ShapeDtypeStruct((b, 2, s), F32),
        ],
        compiler_params=pltpu.CompilerParams(
            dimension_semantics=("arbitrary", "arbitrary"), vmem_limit_bytes=VMEM_LIMIT),
        name="xattn_router",
    )(x1, ln2_w, wq, kmem, vmem, wo, ln3_w, wr_t, br_t)


def _dispatch_kernel(tail_ref, na_ref, d0_ref, d1_ref, h_ref, spare_ref, xs_ref, tgt_ref,
                     zbuf, table, sem, zsem, tsem):
    i = pl.program_id(0)
    tm = MOVE_TILE
    nb = xs_ref.shape[0] // ROW_BLOCK
    n_tok = pl.num_programs(0) * tm

    @pl.when(i == 0)
    def _():
        cp = pltpu.make_async_copy(spare_ref, table, tsem)
        cp.start()
        cp.wait()

    @pl.when(i == 0)
    def _():
        zbuf[...] = jnp.zeros_like(zbuf)

        def zero_copy(start_row):
            start_row = pl.multiple_of(start_row, ROW_BLOCK)
            return pltpu.make_async_copy(zbuf, xs_ref.at[pl.ds(start_row, ROW_BLOCK), :], zsem)

        def start(e, c):
            @pl.when(tail_ref[e] >= 0)
            def _():
                zero_copy(tail_ref[e]).start()
            return c

        def wait(e, c):
            @pl.when(tail_ref[e] >= 0)
            def _():
                zero_copy(tail_ref[e]).wait()
            return c

        def start_unused(b, c):
            zero_copy(b * ROW_BLOCK).start()
            return c

        def wait_unused(b, c):
            zero_copy(b * ROW_BLOCK).wait()
            return c

        lax.fori_loop(0, N_EXPERTS, start, 0)
        lax.fori_loop(na_ref[0], nb, start_unused, 0)
        lax.fori_loop(0, N_EXPERTS, wait, 0)
        lax.fori_loop(na_ref[0], nb, wait_unused, 0)

    def row_copy(t, dst_row):
        return pltpu.make_async_copy(h_ref.at[pl.ds(t, 1), :], xs_ref.at[pl.ds(dst_row, 1), :], sem)

    for t in range(tm):
        r0 = d0_ref[0, t]
        r1 = d1_ref[0, t]
        row_copy(t, r0).start(priority=0)
        row_copy(t, r1).start(priority=1)
        table[r0] = i * tm + t
        table[r1] = n_tok + i * tm + t
    for t in range(tm):
        row_copy(t, 0).wait()
        row_copy(t, 0).wait()

    @pl.when(i == pl.num_programs(0) - 1)
    def _():
        cp = pltpu.make_async_copy(table, tgt_ref, tsem)
        cp.start()
        cp.wait()


def _dispatch_call(tail_start, n_active, dest0, dest1, h3, spare_target):
    t, d = h3.shape
    tm = MOVE_TILE
    nt = t // tm
    n_rows = spare_target.shape[0]
    grid_spec = pltpu.PrefetchScalarGridSpec(
        num_scalar_prefetch=2,
        grid=(nt,),
        in_specs=[
            pl.BlockSpec((None, 1, tm), lambda i, *_: (i, 0, 0), memory_space=pltpu.SMEM),
            pl.BlockSpec((None, 1, tm), lambda i, *_: (i, 0, 0), memory_space=pltpu.SMEM),
            pl.BlockSpec((tm, d), lambda i, *_: (i, 0)),
            pl.BlockSpec(memory_space=pl.ANY),
        ],
        out_specs=[pl.BlockSpec(memory_space=pl.ANY), pl.BlockSpec(memory_space=pl.ANY)],
        scratch_shapes=[
            pltpu.VMEM((ROW_BLOCK, d), F32),
            pltpu.SMEM((n_rows,), jnp.int32),
            pltpu.SemaphoreType.DMA(()),
            pltpu.SemaphoreType.DMA(()),
            pltpu.SemaphoreType.DMA(()),
        ],
    )
    return pl.pallas_call(
        _dispatch_kernel,
        grid_spec=grid_spec,
        out_shape=[jax.ShapeDtypeStruct((n_rows, d), F32),
                   jax.ShapeDtypeStruct((n_rows,), jnp.int32)],
        compiler_params=pltpu.CompilerParams(
            dimension_semantics=("arbitrary",), vmem_limit_bytes=VMEM_LIMIT,
            has_side_effects=True),
        name="dispatch",
    )(tail_start, n_active, dest0.reshape(nt, 1, tm), dest1.reshape(nt, 1, tm), h3, spare_target)


def _experts_kernel(be_ref, na_ref, tgt_ref, xs_ref, wg_ref, wu_ref, wd_ref, out_ref,
                    ybuf, wg_s, wu_s, wd_s, prev_e, sems):
    s = pl.program_id(0)
    na = na_ref[0]
    cur = s % 2

    @pl.when(s == 0)
    def _():
        prev_e[0] = -1

    def row_copy(buf, k, dst_row):
        return pltpu.make_async_copy(
            ybuf.at[buf, pl.ds(k, 1), :], out_ref.at[pl.ds(dst_row, 1), :], sems.at[buf])

    def drain(buf):
        for k in range(ROW_BLOCK):
            row_copy(buf, k, 0).wait()

    def scatter(buf):
        for k in range(ROW_BLOCK):
            row_copy(buf, k, tgt_ref[0, k]).start(priority=k % 2)

    def load_weights():
        e = be_ref[s]

        @pl.when(e != prev_e[0])
        def _():
            wg_s[...] = wg_ref[...].astype(BF16)
            wu_s[...] = wu_ref[...].astype(BF16)
            wd_s[...] = wd_ref[...].astype(BF16)
            prev_e[0] = e

    def compute(buf):
        half = ROW_BLOCK // 2
        for hh in range(2):
            xb = xs_ref[pl.ds(hh * half, half), :].astype(BF16)
            g = jnp.dot(xb, wg_s[...], preferred_element_type=F32)
            u = jnp.dot(xb, wu_s[...], preferred_element_type=F32)
            hmid = (g * jax.nn.sigmoid(g) * u).astype(BF16)
            ybuf[buf, pl.ds(hh * half, half), :] = jnp.dot(
                hmid, wd_s[...], preferred_element_type=F32)

    @pl.when(jnp.logical_and(s >= 2, s - 2 < na))
    def _():
        drain(cur)

    @pl.when(s < na)
    def _():
        load_weights()

    @pl.when(s == 0)
    def _():
        n_slots = out_ref.shape[0] - 2 * ROW_BLOCK
        ybuf[1] = jnp.zeros((ROW_BLOCK, ybuf.shape[2]), F32)
        spare = [pltpu.make_async_copy(
            ybuf.at[1], out_ref.at[pl.ds(n_slots + p * ROW_BLOCK, ROW_BLOCK), :], sems.at[1])
            for p in range(2)]
        for cp in spare:
            cp.start()
        for cp in spare:
            cp.wait()
        compute(cur)

    @pl.when(jnp.logical_and(s >= 1, s < na))
    def _():
        scatter(1 - cur)
        compute(cur)

    @pl.when(s == na)
    def _():
        scatter(1 - cur)


def _experts_call(block_expert, n_active, row_target, xs, w_gate, w_up, w_down, n_out_rows):
    n_rows, d = xs.shape
    nb = n_rows // ROW_BLOCK
    de = w_gate.shape[-1]

    def blk(s, be, na):
        return jnp.minimum(s, na[0] - 1)

    def prev_blk(s, be, na):
        return jnp.clip(s - 1, 0, na[0] - 1)

    grid_spec = pltpu.PrefetchScalarGridSpec(
        num_scalar_prefetch=2,
        grid=(nb + 2,),
        in_specs=[
            pl.BlockSpec((None, 1, ROW_BLOCK), lambda s, be, na: (prev_blk(s, be, na), 0, 0),
                         memory_space=pltpu.SMEM),
            pl.BlockSpec((ROW_BLOCK, d), lambda s, be, na: (blk(s, be, na), 0)),
            pl.BlockSpec((None, d, de), lambda s, be, na: (be[blk(s, be, na)], 0, 0)),
            pl.BlockSpec((None, d, de), lambda s, be, na: (be[blk(s, be, na)], 0, 0)),
            pl.BlockSpec((None, de, d), lambda s, be, na: (be[blk(s, be, na)], 0, 0)),
        ],
        out_specs=pl.BlockSpec(memory_space=pl.ANY),
        scratch_shapes=[
            pltpu.VMEM((2, ROW_BLOCK, d), F32),
            pltpu.VMEM((d, de), BF16),
            pltpu.VMEM((d, de), BF16),
            pltpu.VMEM((de, d), BF16),
            pltpu.SMEM((1,), jnp.int32),
            pltpu.SemaphoreType.DMA((2,)),
        ],
    )
    return pl.pallas_call(
        _experts_kernel,
        grid_spec=grid_spec,
        out_shape=jax.ShapeDtypeStruct((n_out_rows, d), F32),
        compiler_params=pltpu.CompilerParams(
            dimension_semantics=("arbitrary",), vmem_limit_bytes=VMEM_LIMIT,
            has_side_effects=True),
        name="experts",
    )(block_expert, n_active, row_target.reshape(nb, 1, ROW_BLOCK), xs, w_gate, w_up, w_down)


def _combine_kernel(x_ref, y0_ref, y1_ref, cw_ref, fw_ref, o_ref):
    cw = cw_ref[...]
    moe = cw[:, 0:1] * y0_ref[...] + cw[:, 1:2] * y1_ref[...]
    o_ref[...] = _rms(x_ref[...] + moe, fw_ref[...])


def _combine_call(x2, y_slots, cw_t, final_w):
    t, d = x2.shape
    tm = TOK_TILE
    nt = t // tm
    return pl.pallas_call(
        _combine_kernel,
        grid=(nt,),
        in_specs=[
            pl.BlockSpec((tm, d), lambda i: (i, 0)),
            pl.BlockSpec((tm, d), lambda i: (i, 0)),
            pl.BlockSpec((tm, d), lambda i: (nt + i, 0)),
            pl.BlockSpec((tm, 2), lambda i: (i, 0)),
            _const_spec((1, d)),
        ],
        out_specs=pl.BlockSpec((tm, d), lambda i: (i, 0)),
        out_shape=jax.ShapeDtypeStruct((t, d), F32),
        compiler_params=pltpu.CompilerParams(
            dimension_semantics=("arbitrary",), vmem_limit_bytes=VMEM_LIMIT),
        name="combine",
    )(x2, y_slots, y_slots, cw_t, final_w)


def _block_diag(w):
    n, c, d = w.shape
    eye = jnp.eye(n, dtype=w.dtype)
    return (w[:, :, None, :] * eye[:, None, :, None]).reshape(n * c, n * d)


def _rel_bias_pair(rel_bias):
    nh = rel_bias.shape[0]
    far = LEFT_CHUNKS * CHUNK - REL_CLIP
    t = jnp.concatenate(
        [rel_bias[:, REL_CLIP - (CHUNK - 1):],
         jnp.broadcast_to(rel_bias[:, -1:], (nh, far + CHUNK - 1))], axis=1)
    period = t.shape[1] + 1
    tp = jnp.pad(t, ((0, 0), (0, 1)))
    hankel = jnp.tile(tp, (1, CHUNK + 1))[:, :CHUNK * (period + 1)].reshape(nh, CHUNK, period + 1)
    band = hankel[:, :, :BAND][:, :, ::-1]
    neg = jnp.full((nh, CHUNK, CHUNK), NEG_INF, F32)
    return jnp.concatenate(
        [jnp.concatenate([band, neg], axis=2), jnp.concatenate([neg, band], axis=2)], axis=1)


def _routing_plan(eidx, n_blocks):
    flat_e = eidx.reshape(-1)
    onehot = (flat_e[:, None] == jnp.arange(N_EXPERTS, dtype=jnp.int32)[None, :]).astype(jnp.int32)
    csum = jnp.cumsum(onehot, axis=0)
    counts = csum[-1]
    rank = jnp.sum(onehot * csum, axis=1) - 1
    padded = (counts + ROW_BLOCK - 1) // ROW_BLOCK * ROW_BLOCK
    pad_end = jnp.cumsum(padded)
    pad_start = pad_end - padded
    dest = jnp.sum(onehot * pad_start[None, :], axis=1) + rank
    block_row = jnp.arange(n_blocks, dtype=jnp.int32) * ROW_BLOCK
    block_expert = jnp.minimum(
        jnp.sum((pad_end[None, :] <= block_row[:, None]).astype(jnp.int32), axis=1), N_EXPERTS - 1)
    n_active = (pad_end[-1] // ROW_BLOCK).astype(jnp.int32).reshape(1)
    tail_start = jnp.where(padded > 0, pad_end - ROW_BLOCK, -1).astype(jnp.int32)
    dest = dest.astype(jnp.int32)
    n_slots = flat_e.shape[0]
    rows = jnp.arange(n_blocks * ROW_BLOCK, dtype=jnp.int32)
    spare_target = n_slots + (rows // ROW_BLOCK % 2) * ROW_BLOCK + rows % ROW_BLOCK
    return dest.reshape(eidx.shape), block_expert, n_active, tail_start, spare_target


def kernel(x, mem, ln1_w, w_in, conv_w, conv_b, rnn_wa, rnn_ba, rnn_wx, rnn_bx, rnn_lambda, rel_bias, gn_rnn_w, gn_att_w, w_out, ln2_w, mem_norm_w, xq_w, xk_w, xv_w, xo_w, ln3_w, router_group_w, router_group_b, router_expert_w, router_expert_b, expert_gate_w, expert_up_w, expert_down_w, final_norm_w):
    b, s, d = x.shape
    t = b * s
    depth = ln1_w.shape[0]
    row = lambda v: v.reshape(1, -1)

    kmem, vmem = None, None
    for l in range(depth):
        x = _mixer_call(
            x, row(ln1_w[l]), w_in[l].astype(BF16),
            w_in[l][:, 2 * D_RNN + D_ATT:2 * D_RNN + 2 * D_ATT].T.astype(BF16),
            conv_w[l], row(conv_b[l]),
            _block_diag(rnn_wa[l]).astype(BF16), row(rnn_ba[l]),
            _block_diag(rnn_wx[l]).astype(BF16), row(rnn_bx[l]), row(rnn_lambda[l]),
            _rel_bias_pair(rel_bias[l]), row(gn_rnn_w[l]), row(gn_att_w[l]), w_out[l].astype(BF16))

        kmem, vmem = _memkv_call(mem, row(mem_norm_w), xk_w[l].astype(BF16), xv_w[l].astype(BF16))
        pad = jnp.zeros((SUBLANES - N_GROUPS, d), F32)
        wr_t = jnp.concatenate(
            [router_group_w[l].T, pad,
             router_expert_w[l].transpose(0, 2, 1).reshape(N_EXPERTS, d)], axis=0).astype(BF16)
        br_t = jnp.concatenate(
            [router_group_b[l], jnp.zeros((SUBLANES - N_GROUPS,), F32),
             router_expert_b[l].reshape(-1)]).reshape(ROUTER_ROWS, 1)
        x2, h3, eidx, cw = _xattn_router_call(
            x, row(ln2_w[l]), xq_w[l].astype(BF16), kmem, vmem, xo_w[l].astype(BF16),
            row(ln3_w[l]), wr_t, br_t)

        n_blocks = (2 * t) // ROW_BLOCK + N_EXPERTS
        eidx_t = eidx.transpose(1, 0, 2).reshape(2, t)
        dest, block_expert, n_active, tail_start, spare_target = _routing_plan(eidx_t, n_blocks)
        xs, row_target = _dispatch_call(tail_start, n_active, dest[0], dest[1], h3.reshape(t, d),
                                        spare_target)
        y_slots = _experts_call(block_expert, n_active, row_target, xs, expert_gate_w[l],
                                expert_up_w[l], expert_down_w[l], 2 * t + 2 * ROW_BLOCK)
        cw_t = cw.transpose(0, 2, 1).reshape(t, 2)
        is_last = l == depth - 1
        assert is_last, "only a single layer is supported"
        x = _combine_call(x2.reshape(t, d), y_slots, cw_t, row(final_norm_w))
        x = x.reshape(b, s, d)
    return x
```

```python
import functools

import jax
import jax.numpy as jnp
import numpy as np
from jax import lax
from jax.experimental import pallas as pl
from jax.experimental.pallas import tpu as pltpu

F32 = jnp.float32
BF16 = jnp.bfloat16

D_MODEL = 1024
D_RNN = 512
D_ATT = 512
N_RNN_BLOCKS = 8
CONV_WIDTH = 4
RG_LRU_C = 8.0
N_ATT_HEADS = 8
ATT_HEAD_DIM = 64
CHUNK = 64
LEFT_CHUNKS = 8
BAND = (LEFT_CHUNKS + 1) * CHUNK
PAIR_ROWS = 2 * CHUNK
PAIR_KEYS = BAND + CHUNK
REL_CLIP = 128
N_XHEADS = 4
XHEAD_DIM = 256
N_GROUPS = 4
EXPERTS_PER_GROUP = 8
N_EXPERTS = 32
D_EXPERT = 512
EPS = 1e-6
NEG_INF = -1e30

SUBLANES = 8
SEQ_TILE = LEFT_CHUNKS * CHUNK
TOK_TILE = 512
ROW_BLOCK = 512
MOVE_TILE = 256
ROUTER_ROWS = 40
VMEM_LIMIT = 56 * 1024 * 1024


def _rms(x, w):
    return x * lax.rsqrt(jnp.mean(x * x, axis=-1, keepdims=True) + EPS) * w


def _const_spec(shape):
    return pl.BlockSpec(shape, lambda *_: (0,) * len(shape))


def _memkv_kernel(mem_ref, w_ref, wk_ref, wv_ref, k_ref, v_ref):
    mn = _rms(mem_ref[...], w_ref[...]).astype(BF16)
    k_ref[...] = jnp.dot(mn, wk_ref[...], preferred_element_type=F32).astype(BF16)
    v_ref[...] = jnp.dot(mn, wv_ref[...], preferred_element_type=F32).astype(BF16)


def _memkv_call(mem, mem_norm_w, wk, wv):
    b, m, d = mem.shape
    return pl.pallas_call(
        _memkv_kernel,
        grid=(b,),
        in_specs=[
            pl.BlockSpec((None, m, d), lambda i: (i, 0, 0)),
            _const_spec((1, d)),
            _const_spec((d, d)),
            _const_spec((d, d)),
        ],
        out_specs=[
            pl.BlockSpec((None, m, d), lambda i: (i, 0, 0)),
            pl.BlockSpec((None, m, d), lambda i: (i, 0, 0)),
        ],
        out_shape=[jax.ShapeDtypeStruct((b, m, d), BF16)] * 2,
        compiler_params=pltpu.CompilerParams(vmem_limit_bytes=VMEM_LIMIT),
        name="mem_kv",
    )(mem, mem_norm_w, wk, wv)


def _mixer_kernel(x_ref, ln1_ref, win_ref, wkt_ref, convw_ref, convb_ref, wa_ref, ba_ref, wx_ref,
                  bx_ref, lam_ref, bias_ref, gnr_ref, gna_ref, wout_ref, o_ref,
                  xr_ext, h_state, q_s, kt_s, vbuf, yatt):
    ts = SEQ_TILE
    i = pl.program_id(1)

    @pl.when(i == 0)
    def _():
        xr_ext[pl.ds(0, SUBLANES), :] = jnp.zeros((SUBLANES, D_RNN), F32)
        h_state[...] = jnp.zeros_like(h_state)
        kt_s[:, pl.ds(0, ts)] = jnp.zeros((D_ATT, ts), BF16)
        vbuf[pl.ds(0, ts), :] = jnp.zeros((ts, D_ATT), BF16)

    x = x_ref[...]
    h = _rms(x, ln1_ref[...]).astype(BF16)

    def proj(lo):
        return jnp.dot(h, win_ref[:, lo:lo + D_RNN], preferred_element_type=F32)

    xr_ext[pl.ds(SUBLANES, ts), :] = proj(0)
    xc = convb_ref[...] + sum(
        convw_ref[j:j + 1, :] * xr_ext[pl.ds(SUBLANES - (CONV_WIDTH - 1) + j, ts), :]
        for j in range(CONV_WIDTH))
    xr_ext[pl.ds(0, SUBLANES), :] = xr_ext[pl.ds(ts, SUBLANES), :]
    xcb = xc.astype(BF16)
    r = jax.nn.sigmoid(jnp.dot(xcb, wa_ref[...], preferred_element_type=F32) + ba_ref[...])
    gate_i = jax.nn.sigmoid(jnp.dot(xcb, wx_ref[...], preferred_element_type=F32) + bx_ref[...])
    z = -lam_ref[...]
    softplus = jnp.maximum(z, 0.0) + jnp.log1p(jnp.exp(-jnp.abs(z)))
    log_a = (-RG_LRU_C * r) * softplus
    a = jnp.exp(log_a)
    t = jnp.tanh(log_a)
    u = jnp.sqrt(-2.0 * t / (1.0 - t)) * (gate_i * xc)

    ng = ts // SUBLANES
    a3 = a.reshape(ng, SUBLANES, D_RNN)
    b3 = u.reshape(ng, SUBLANES, D_RNN)
    row = lax.broadcasted_iota(jnp.int32, (ng, SUBLANES, D_RNN), 1)
    for d in (1, 2, 4):
        a_sh = pltpu.roll(a3, d, 1)
        b_sh = pltpu.roll(b3, d, 1)
        m = row >= d
        b3 = jnp.where(m, a3 * b_sh + b3, b3)
        a3 = jnp.where(m, a3 * a_sh, a3)

    hprev = h_state[...]
    hs = []
    for g in range(ng):
        hg = a3[g] * hprev + b3[g]
        hs.append(hg)
        hprev = jnp.broadcast_to(hg[SUBLANES - 1:SUBLANES, :], (SUBLANES, D_RNN))
    h_state[...] = hprev
    hseq = jnp.concatenate(hs, axis=0)
    xg = proj(D_RNN)
    gelu = 0.5 * xg * (1.0 + jnp.tanh(0.7978845608028654 * (xg + 0.044715 * (xg * xg * xg))))
    y_rnn = _rms(gelu * hseq, gnr_ref[...]).astype(BF16)

    q_s[...] = (proj(2 * D_RNN) * (ATT_HEAD_DIM ** -0.5)).astype(BF16)
    kt_s[:, pl.ds(ts, ts)] = lax.dot_general(
        wkt_ref[...], h, (((1,), (1,)), ((), ())), preferred_element_type=F32).astype(BF16)
    vbuf[pl.ds(ts, ts), :] = proj(2 * D_RNN + 2 * D_ATT).astype(BF16)
    lane = lax.broadcasted_iota(jnp.int32, (PAIR_ROWS, PAIR_ROWS), 1)
    first_head = lane < ATT_HEAD_DIM
    key_pos = lax.broadcasted_iota(jnp.int32, (2 * PAIR_ROWS, PAIR_KEYS), 1)
    zero_q = jnp.zeros((PAIR_ROWS, PAIR_ROWS), BF16)

    for jj in range(ts // PAIR_ROWS):
        w0 = jj * PAIR_ROWS
        valid2 = key_pos >= jnp.where(i > 0, 0, ts - w0)
        scores = []
        for hp in range(N_ATT_HEADS // 2):
            fl = slice(hp * PAIR_ROWS, (hp + 1) * PAIR_ROWS)
            q2 = q_s[pl.ds(w0, PAIR_ROWS), fl]
            qm = jnp.concatenate(
                [jnp.where(first_head, q2, zero_q), jnp.where(first_head, zero_q, q2)], axis=0)
            scores.append(jnp.dot(qm, kt_s[fl, pl.ds(w0, PAIR_KEYS)], preferred_element_type=F32))
        outs = []
        for hp in range(N_ATT_HEADS // 2):
            fl = slice(hp * PAIR_ROWS, (hp + 1) * PAIR_ROWS)
            bias2 = jnp.concatenate([bias_ref[2 * hp], bias_ref[2 * hp + 1]], axis=0)
            s = jnp.where(valid2, scores[hp] + bias2, NEG_INF)
            p = jnp.exp(s - jnp.max(s, axis=-1, keepdims=True))
            l = jnp.sum(p, axis=-1, keepdims=True)
            o = jnp.dot(p.astype(BF16), vbuf[pl.ds(w0, PAIR_KEYS), fl],
                        preferred_element_type=F32) / l
            outs.append(jnp.where(first_head, o[:PAIR_ROWS], o[PAIR_ROWS:]))
        yatt[pl.ds(w0, PAIR_ROWS), :] = jnp.concatenate(outs, axis=-1)

    kt_s[:, pl.ds(0, ts)] = kt_s[:, pl.ds(ts, ts)]
    vbuf[pl.ds(0, ts), :] = vbuf[pl.ds(ts, ts), :]
    y_att = _rms(yatt[...], gna_ref[...]).astype(BF16)

    mix = (jnp.dot(y_rnn, wout_ref[pl.ds(0, D_RNN), :], preferred_element_type=F32)
           + jnp.dot(y_att, wout_ref[pl.ds(D_RNN, D_ATT), :], preferred_element_type=F32))
    o_ref[...] = x + mix


def _mixer_call(x, ln1_w, w_in, wk_t, conv_w, conv_b, wa_bd, ba, wx_bd, bx, lam, bias_band, gnr, gna,
                w_out):
    b, s, d = x.shape
    ts = SEQ_TILE
    assert s % ts == 0
    dp = w_in.shape[1]
    return pl.pallas_call(
        _mixer_kernel,
        grid=(b, s // ts),
        in_specs=[
            pl.BlockSpec((None, ts, d), lambda bi, i: (bi, i, 0)),
            _const_spec((1, d)),
            _const_spec((d, dp)),
            _const_spec((D_ATT, d)),
            _const_spec((CONV_WIDTH, D_RNN)),
            _const_spec((1, D_RNN)),
            _const_spec((D_RNN, D_RNN)),
            _const_spec((1, D_RNN)),
            _const_spec((D_RNN, D_RNN)),
            _const_spec((1, D_RNN)),
            _const_spec((1, D_RNN)),
            _const_spec((N_ATT_HEADS, PAIR_ROWS, PAIR_KEYS)),
            _const_spec((1, D_RNN)),
            _const_spec((1, D_ATT)),
            _const_spec((d, d)),
        ],
        out_specs=pl.BlockSpec((None, ts, d), lambda bi, i: (bi, i, 0)),
        out_shape=jax.ShapeDtypeStruct((b, s, d), F32),
        scratch_shapes=[
            pltpu.VMEM((ts + SUBLANES, D_RNN), F32),
            pltpu.VMEM((SUBLANES, D_RNN), F32),
            pltpu.VMEM((ts, D_ATT), BF16),
            pltpu.VMEM((D_ATT, 2 * ts), BF16),
            pltpu.VMEM((2 * ts, D_ATT), BF16),
            pltpu.VMEM((ts, D_ATT), F32),
        ],
        compiler_params=pltpu.CompilerParams(
            dimension_semantics=("arbitrary", "arbitrary"), vmem_limit_bytes=VMEM_LIMIT),
        name="mixer",
    )(x, ln1_w, w_in, wk_t, conv_w, conv_b, wa_bd, ba, wx_bd, bx, lam, bias_band, gnr, gna, w_out)


def _xattn_router_kernel(x_ref, ln2_ref, wq_ref, k_ref, v_ref, wo_ref, ln3_ref, wr_ref, br_ref,
                         x2_ref, h3_ref, eidx_ref, cw_ref):
    x = x_ref[...]
    h = _rms(x, ln2_ref[...]).astype(BF16)
    q = (jnp.dot(h, wq_ref[...], preferred_element_type=F32) * (XHEAD_DIM ** -0.5)).astype(BF16)
    outs = []
    for hd in range(N_XHEADS):
        sl = slice(hd * XHEAD_DIM, (hd + 1) * XHEAD_DIM)
        s = lax.dot_general(q[:, sl], k_ref[:, sl], (((1,), (1,)), ((), ())),
                            preferred_element_type=F32)
        p = jnp.exp(s - jnp.max(s, axis=-1, keepdims=True))
        l = jnp.sum(p, axis=-1, keepdims=True)
        o = jnp.dot(p.astype(BF16), v_ref[:, sl], preferred_element_type=F32)
        outs.append((o / l).astype(BF16))
    att = jnp.concatenate(outs, axis=-1)
    x2 = x + jnp.dot(att, wo_ref[...], preferred_element_type=F32)
    x2_ref[...] = x2

    h3 = _rms(x2, ln3_ref[...])
    h3_ref[...] = h3
    logits = lax.dot_general(wr_ref[...], h3.astype(BF16), (((1,), (1,)), ((), ())),
                             preferred_element_type=F32) + br_ref[...]
    gl = logits[0:N_GROUPS, :]
    ge = jnp.exp(gl - jnp.max(gl, axis=0, keepdims=True))
    gp = ge / jnp.sum(ge, axis=0, keepdims=True)
    g_p = jnp.max(gp, axis=0, keepdims=True)
    g_iota = lax.broadcasted_iota(jnp.int32, gp.shape, 0)
    g_idx = jnp.min(jnp.where(gp == g_p, g_iota, N_GROUPS), axis=0, keepdims=True)

    el = jnp.zeros((EXPERTS_PER_GROUP, logits.shape[1]), F32)
    for g in range(N_GROUPS):
        lo = SUBLANES + g * EXPERTS_PER_GROUP
        el = jnp.where(g_idx == g, logits[lo:lo + EXPERTS_PER_GROUP, :], el)
    ee = jnp.exp(el - jnp.max(el, axis=0, keepdims=True))
    ep = ee / jnp.sum(ee, axis=0, keepdims=True)
    e_iota = lax.broadcasted_iota(jnp.int32, ep.shape, 0)
    p1 = jnp.max(ep, axis=0, keepdims=True)
    i1 = jnp.min(jnp.where(ep == p1, e_iota, EXPERTS_PER_GROUP), axis=0, keepdims=True)
    ep2 = jnp.where(e_iota == i1, -1.0, ep)
    p2 = jnp.max(ep2, axis=0, keepdims=True)
    i2 = jnp.min(jnp.where(ep2 == p2, e_iota, EXPERTS_PER_GROUP), axis=0, keepdims=True)
    den = p1 + p2
    eidx_ref[...] = jnp.concatenate(
        [g_idx * EXPERTS_PER_GROUP + i1, g_idx * EXPERTS_PER_GROUP + i2], axis=0)
    cw_ref[...] = jnp.concatenate([g_p * (p1 / den), g_p * (p2 / den)], axis=0)


def _xattn_router_call(x1, ln2_w, wq, kmem, vmem, wo, ln3_w, wr_t, br_t):
    b, s, d = x1.shape
    tt = TOK_TILE
    m = kmem.shape[1]
    return pl.pallas_call(
        _xattn_router_kernel,
        grid=(b, s // tt),
        in_specs=[
            pl.BlockSpec((None, tt, d), lambda bi, i: (bi, i, 0)),
            _const_spec((1, d)),
            _const_spec((d, d)),
            pl.BlockSpec((None, m, d), lambda bi, i: (bi, 0, 0)),
            pl.BlockSpec((None, m, d), lambda bi, i: (bi, 0, 0)),
            _const_spec((d, d)),
            _const_spec((1, d)),
            _const_spec((ROUTER_ROWS, d)),
            _const_spec((ROUTER_ROWS, 1)),
        ],
        out_specs=[
            pl.BlockSpec((None, tt, d), lambda bi, i: (bi, i, 0)),
            pl.BlockSpec((None, tt, d), lambda bi, i: (bi, i, 0)),
            pl.BlockSpec((None, 2, tt), lambda bi, i: (bi, 0, i)),
            pl.BlockSpec((None, 2, tt), lambda bi, i: (bi, 0, i)),
        ],
        out_shape=[
            jax.ShapeDtypeStruct((b, s, d), F32),
            jax.ShapeDtypeStruct((b, s, d), F32),
            jax.ShapeDtypeStruct((b, 2, s), jnp.int32),
            jax.ShapeDtypeStruct((b, 2, s), F32),
        ],
        compiler_params=pltpu.CompilerParams(
            dimension_semantics=("arbitrary", "arbitrary"), vmem_limit_bytes=VMEM_LIMIT),
        name="xattn_router",
    )(x1, ln2_w, wq, kmem, vmem, wo, ln3_w, wr_t, br_t)


def _dispatch_kernel(tail_ref, na_ref, d0_ref, d1_ref, h_ref, spare_ref, xs_ref, tgt_ref,
                     zbuf, table, sem, zsem, tsem):
    i = pl.program_id(0)
    tm = MOVE_TILE
    nb = xs_ref.shape[0] // ROW_BLOCK
    n_tok = pl.num_programs(0) * tm

    @pl.when(i == 0)
    def _():
        cp = pltpu.make_async_copy(spare_ref, table, tsem)
        cp.start()
        cp.wait()

    @pl.when(i == 0)
    def _():
        zbuf[...] = jnp.zeros_like(zbuf)

        def zero_copy(start_row):
            start_row = pl.multiple_of(start_row, ROW_BLOCK)
            return pltpu.make_async_copy(zbuf, xs_ref.at[pl.ds(start_row, ROW_BLOCK), :], zsem)

        def start(e, c):
            @pl.when(tail_ref[e] >= 0)
            def _():
                zero_copy(tail_ref[e]).start()
            return c

        def wait(e, c):
            @pl.when(tail_ref[e] >= 0)
            def _():
                zero_copy(tail_ref[e]).wait()
            return c

        def start_unused(b, c):
            zero_copy(b * ROW_BLOCK).start()
            return c

        def wait_unused(b, c):
            zero_copy(b * ROW_BLOCK).wait()
            return c

        lax.fori_loop(0, N_EXPERTS, start, 0)
        lax.fori_loop(na_ref[0], nb, start_unused, 0)
        lax.fori_loop(0, N_EXPERTS, wait, 0)
        lax.fori_loop(na_ref[0], nb, wait_unused, 0)

    def row_copy(t, dst_row):
        return pltpu.make_async_copy(h_ref.at[pl.ds(t, 1), :], xs_ref.at[pl.ds(dst_row, 1), :], sem)

    for t in range(tm):
        r0 = d0_ref[0, t]
        r1 = d1_ref[0, t]
        row_copy(t, r0).start(priority=0)
        row_copy(t, r1).start(priority=1)
        table[r0] = i * tm + t
        table[r1] = n_tok + i * tm + t
    for t in range(tm):
        row_copy(t, 0).wait()
        row_copy(t, 0).wait()

    @pl.when(i == pl.num_programs(0) - 1)
    def _():
        cp = pltpu.make_async_copy(table, tgt_ref, tsem)
        cp.start()
        cp.wait()


def _dispatch_call(tail_start, n_active, dest0, dest1, h3, spare_target):
    t, d = h3.shape
    tm = MOVE_TILE
    nt = t // tm
    n_rows = spare_target.shape[0]
    grid_spec = pltpu.PrefetchScalarGridSpec(
        num_scalar_prefetch=2,
        grid=(nt,),
        in_specs=[
            pl.BlockSpec((None, 1, tm), lambda i, *_: (i, 0, 0), memory_space=pltpu.SMEM),
            pl.BlockSpec((None, 1, tm), lambda i, *_: (i, 0, 0), memory_space=pltpu.SMEM),
            pl.BlockSpec((tm, d), lambda i, *_: (i, 0)),
            pl.BlockSpec(memory_space=pl.ANY),
        ],
        out_specs=[pl.BlockSpec(memory_space=pl.ANY), pl.BlockSpec(memory_space=pl.ANY)],
        scratch_shapes=[
            pltpu.VMEM((ROW_BLOCK, d), F32),
            pltpu.SMEM((n_rows,), jnp.int32),
            pltpu.SemaphoreType.DMA(()),
            pltpu.SemaphoreType.DMA(()),
            pltpu.SemaphoreType.DMA(()),
        ],
    )
    return pl.pallas_call(
        _dispatch_kernel,
        grid_spec=grid_spec,
        out_shape=[jax.ShapeDtypeStruct((n_rows, d), F32),
                   jax.ShapeDtypeStruct((n_rows,), jnp.int32)],
        compiler_params=pltpu.CompilerParams(
            dimension_semantics=("arbitrary",), vmem_limit_bytes=VMEM_LIMIT,
            has_side_effects=True),
        name="dispatch",
    )(tail_start, n_active, dest0.reshape(nt, 1, tm), dest1.reshape(nt, 1, tm), h3, spare_target)


def _experts_kernel(be_ref, na_ref, tgt_ref, xs_ref, wg_ref, wu_ref, wd_ref, out_ref,
                    ybuf, wg_s, wu_s, wd_s, prev_e, sems):
    s = pl.program_id(0)
    na = na_ref[0]
    cur = s % 2

    @pl.when(s == 0)
    def _():
        prev_e[0] = -1

    def row_copy(buf, k, dst_row):
        return pltpu.make_async_copy(
            ybuf.at[buf, pl.ds(k, 1), :], out_ref.at[pl.ds(dst_row, 1), :], sems.at[buf])

    def drain(buf):
        for k in range(ROW_BLOCK):
            row_copy(buf, k, 0).wait()

    def scatter(buf):
        for k in range(ROW_BLOCK):
            row_copy(buf, k, tgt_ref[0, k]).start(priority=k % 2)

    def load_weights():
        e = be_ref[s]

        @pl.when(e != prev_e[0])
        def _():
            wg_s[...] = wg_ref[...].astype(BF16)
            wu_s[...] = wu_ref[...].astype(BF16)
            wd_s[...] = wd_ref[...].astype(BF16)
            prev_e[0] = e

    def compute(buf):
        xb = xs_ref[...].astype(BF16)
        g = jnp.dot(xb, wg_s[...], preferred_element_type=F32)
        u = jnp.dot(xb, wu_s[...], preferred_element_type=F32)
        hmid = (g * jax.nn.sigmoid(g) * u).astype(BF16)
        ybuf[buf] = jnp.dot(hmid, wd_s[...], preferred_element_type=F32)

    @pl.when(jnp.logical_and(s >= 2, s - 2 < na))
    def _():
        drain(cur)

    @pl.when(s < na)
    def _():
        load_weights()

    @pl.when(s == 0)
    def _():
        n_slots = out_ref.shape[0] - 2 * ROW_BLOCK
        ybuf[1] = jnp.zeros((ROW_BLOCK, ybuf.shape[2]), F32)
        spare = [pltpu.make_async_copy(
            ybuf.at[1], out_ref.at[pl.ds(n_slots + p * ROW_BLOCK, ROW_BLOCK), :], sems.at[1])
            for p in range(2)]
        for cp in spare:
            cp.start()
        for cp in spare:
            cp.wait()
        compute(cur)

    @pl.when(jnp.logical_and(s >= 1, s < na))
    def _():
        scatter(1 - cur)
        compute(cur)

    @pl.when(s == na)
    def _():
        scatter(1 - cur)


def _experts_call(block_expert, n_active, row_target, xs, w_gate, w_up, w_down, n_out_rows):
    n_rows, d = xs.shape
    nb = n_rows // ROW_BLOCK
    de = w_gate.shape[-1]

    def blk(s, be, na):
        return jnp.minimum(s, na[0] - 1)

    def prev_blk(s, be, na):
        return jnp.clip(s - 1, 0, na[0] - 1)

    grid_spec = pltpu.PrefetchScalarGridSpec(
        num_scalar_prefetch=2,
        grid=(nb + 2,),
        in_specs=[
            pl.BlockSpec((None, 1, ROW_BLOCK), lambda s, be, na: (prev_blk(s, be, na), 0, 0),
                         memory_space=pltpu.SMEM),
            pl.BlockSpec((ROW_BLOCK, d), lambda s, be, na: (blk(s, be, na), 0)),
            pl.BlockSpec((None, d, de), lambda s, be, na: (be[blk(s, be, na)], 0, 0)),
            pl.BlockSpec((None, d, de), lambda s, be, na: (be[blk(s, be, na)], 0, 0)),
            pl.BlockSpec((None, de, d), lambda s, be, na: (be[blk(s, be, na)], 0, 0)),
        ],
        out_specs=pl.BlockSpec(memory_space=pl.ANY),
        scratch_shapes=[
            pltpu.VMEM((2, ROW_BLOCK, d), F32),
            pltpu.VMEM((d, de), BF16),
            pltpu.VMEM((d, de), BF16),
            pltpu.VMEM((de, d), BF16),
            pltpu.SMEM((1,), jnp.int32),
            pltpu.SemaphoreType.DMA((2,)),
        ],
    )
    return pl.pallas_call(
        _experts_kernel,
        grid_spec=grid_spec,
        out_shape=jax.ShapeDtypeStruct((n_out_rows, d), F32),
        compiler_params=pltpu.CompilerParams(
            dimension_semantics=("arbitrary",), vmem_limit_bytes=VMEM_LIMIT,
            has_side_effects=True),
        name="experts",
    )(block_expert, n_active, row_target.reshape(nb, 1, ROW_BLOCK), xs, w_gate, w_up, w_down)


def _combine_kernel(x_ref, y0_ref, y1_ref, cw_ref, fw_ref, o_ref):
    cw = cw_ref[...]
    moe = cw[:, 0:1] * y0_ref[...] + cw[:, 1:2] * y1_ref[...]
    o_ref[...] = _rms(x_ref[...] + moe, fw_ref[...])


def _combine_call(x2, y_slots, cw_t, final_w):
    t, d = x2.shape
    tm = TOK_TILE
    nt = t // tm
    return pl.pallas_call(
        _combine_kernel,
        grid=(nt,),
        in_specs=[
            pl.BlockSpec((tm, d), lambda i: (i, 0)),
            pl.BlockSpec((tm, d), lambda i: (i, 0)),
            pl.BlockSpec((tm, d), lambda i: (nt + i, 0)),
            pl.BlockSpec((tm, 2), lambda i: (i, 0)),
            _const_spec((1, d)),
        ],
        out_specs=pl.BlockSpec((tm, d), lambda i: (i, 0)),
        out_shape=jax.ShapeDtypeStruct((t, d), F32),
        compiler_params=pltpu.CompilerParams(
            dimension_semantics=("arbitrary",), vmem_limit_bytes=VMEM_LIMIT),
        name="combine",
    )(x2, y_slots, y_slots, cw_t, final_w)


def _block_diag(w):
    n, c, d = w.shape
    eye = jnp.eye(n, dtype=w.dtype)
    return (w[:, :, None, :] * eye[:, None, :, None]).reshape(n * c, n * d)


def _rel_bias_pair(rel_bias):
    nh = rel_bias.shape[0]
    far = LEFT_CHUNKS * CHUNK - REL_CLIP
    t = jnp.concatenate(
        [rel_bias[:, REL_CLIP - (CHUNK - 1):],
         jnp.broadcast_to(rel_bias[:, -1:], (nh, far + CHUNK - 1))], axis=1)
    period = t.shape[1] + 1
    tp = jnp.pad(t, ((0, 0), (0, 1)))
    hankel = jnp.tile(tp, (1, CHUNK + 1))[:, :CHUNK * (period + 1)].reshape(nh, CHUNK, period + 1)
    band = hankel[:, :, :BAND][:, :, ::-1]
    neg = jnp.full((nh, CHUNK, CHUNK), NEG_INF, F32)
    return jnp.concatenate(
        [jnp.concatenate([band, neg], axis=2), jnp.concatenate([neg, band], axis=2)], axis=1)


def _routing_plan(eidx, n_blocks):
    flat_e = eidx.reshape(-1)
    onehot = (flat_e[:, None] == jnp.arange(N_EXPERTS, dtype=jnp.int32)[None, :]).astype(jnp.int32)
    csum = jnp.cumsum(onehot, axis=0)
    counts = csum[-1]
    rank = jnp.sum(onehot * csum, axis=1) - 1
    padded = (counts + ROW_BLOCK - 1) // ROW_BLOCK * ROW_BLOCK
    pad_end = jnp.cumsum(padded)
    pad_start = pad_end - padded
    dest = jnp.sum(onehot * pad_start[None, :], axis=1) + rank
    block_row = jnp.arange(n_blocks, dtype=jnp.int32) * ROW_BLOCK
    block_expert = jnp.minimum(
        jnp.sum((pad_end[None, :] <= block_row[:, None]).astype(jnp.int32), axis=1), N_EXPERTS - 1)
    n_active = (pad_end[-1] // ROW_BLOCK).astype(jnp.int32).reshape(1)
    tail_start = jnp.where(padded > 0, pad_end - ROW_BLOCK, -1).astype(jnp.int32)
    dest = dest.astype(jnp.int32)
    n_slots = flat_e.shape[0]
    rows = jnp.arange(n_blocks * ROW_BLOCK, dtype=jnp.int32)
    spare_target = n_slots + (rows // ROW_BLOCK % 2) * ROW_BLOCK + rows % ROW_BLOCK
    return dest.reshape(eidx.shape), block_expert, n_active, tail_start, spare_target


def kernel(x, mem, ln1_w, w_in, conv_w, conv_b, rnn_wa, rnn_ba, rnn_wx, rnn_bx, rnn_lambda, rel_bias, gn_rnn_w, gn_att_w, w_out, ln2_w, mem_norm_w, xq_w, xk_w, xv_w, xo_w, ln3_w, router_group_w, router_group_b, router_expert_w, router_expert_b, expert_gate_w, expert_up_w, expert_down_w, final_norm_w):
    b, s, d = x.shape
    t = b * s
    depth = ln1_w.shape[0]
    row = lambda v: v.reshape(1, -1)

    kmem, vmem = None, None
    for l in range(depth):
        x = _mixer_call(
            x, row(ln1_w[l]), w_in[l].astype(BF16),
            w_in[l][:, 2 * D_RNN + D_ATT:2 * D_RNN + 2 * D_ATT].T.astype(BF16),
            conv_w[l], row(conv_b[l]),
            _block_diag(rnn_wa[l]).astype(BF16), row(rnn_ba[l]),
            _block_diag(rnn_wx[l]).astype(BF16), row(rnn_bx[l]), row(rnn_lambda[l]),
            _rel_bias_pair(rel_bias[l]), row(gn_rnn_w[l]), row(gn_att_w[l]), w_out[l].astype(BF16))

        kmem, vmem = _memkv_call(mem, row(mem_norm_w), xk_w[l].astype(BF16), xv_w[l].astype(BF16))
        pad = jnp.zeros((SUBLANES - N_GROUPS, d), F32)
        wr_t = jnp.concatenate(
            [router_group_w[l].T, pad,
             router_expert_w[l].transpose(0, 2, 1).reshape(N_EXPERTS, d)], axis=0).astype(BF16)
        br_t = jnp.concatenate(
            [router_group_b[l], jnp.zeros((SUBLANES - N_GROUPS,), F32),
             router_expert_b[l].reshape(-1)]).reshape(ROUTER_ROWS, 1)
        x2, h3, eidx, cw = _xattn_router_call(
            x, row(ln2_w[l]), xq_w[l].astype(BF16), kmem, vmem, xo_w[l].astype(BF16),
            row(ln3_w[l]), wr_t, br_t)

        n_blocks = (2 * t) // ROW_BLOCK + N_EXPERTS
        eidx_t = eidx.transpose(1, 0, 2).reshape(2, t)
        dest, block_expert, n_active, tail_start, spare_target = _routing_plan(eidx_t, n_blocks)
        xs, row_target = _dispatch_call(tail_start, n_active, dest[0], dest[1], h3.reshape(t, d),
                                        spare_target)
        y_slots = _experts_call(block_expert, n_active, row_target, xs, expert_gate_w[l],
                                expert_up_w[l], expert_down_w[l], 2 * t + 2 * ROW_BLOCK)
        cw_t = cw.transpose(0, 2, 1).reshape(t, 2)
        is_last = l == depth - 1
        assert is_last, "only a single layer is supported"
        x = _combine_call(x2.reshape(t, d), y_slots, cw_t, row(final_norm_w))
        x = x.reshape(b, s, d)
    return x
```

```python
import functools

import jax
import jax.numpy as jnp
import numpy as np
from jax import lax
from jax.experimental import pallas as pl
from jax.experimental.pallas import tpu as pltpu

F32 = jnp.float32
BF16 = jnp.bfloat16

D_MODEL = 1024
D_RNN = 512
D_ATT = 512
N_RNN_BLOCKS = 8
CONV_WIDTH = 4
RG_LRU_C = 8.0
N_ATT_HEADS = 8
ATT_HEAD_DIM = 64
CHUNK = 64
LEFT_CHUNKS = 8
BAND = (LEFT_CHUNKS + 1) * CHUNK
PAIR_ROWS = 2 * CHUNK
PAIR_KEYS = BAND + CHUNK
REL_CLIP = 128
N_XHEADS = 4
XHEAD_DIM = 256
N_GROUPS = 4
EXPERTS_PER_GROUP = 8
N_EXPERTS = 32
D_EXPERT = 512
EPS = 1e-6
NEG_INF = -1e30

SUBLANES = 8
SEQ_TILE = LEFT_CHUNKS * CHUNK
TOK_TILE = 512
ROW_BLOCK = 512
ROUTER_ROWS = 40
VMEM_LIMIT = 56 * 1024 * 1024


def _rms(x, w):
    return x * lax.rsqrt(jnp.mean(x * x, axis=-1, keepdims=True) + EPS) * w


def _const_spec(shape):
    return pl.BlockSpec(shape, lambda *_: (0,) * len(shape))


def _memkv_kernel(mem_ref, w_ref, wk_ref, wv_ref, k_ref, v_ref):
    mn = _rms(mem_ref[...], w_ref[...]).astype(BF16)
    k_ref[...] = jnp.dot(mn, wk_ref[...], preferred_element_type=F32).astype(BF16)
    v_ref[...] = jnp.dot(mn, wv_ref[...], preferred_element_type=F32).astype(BF16)


def _memkv_call(mem, mem_norm_w, wk, wv):
    b, m, d = mem.shape
    return pl.pallas_call(
        _memkv_kernel,
        grid=(b,),
        in_specs=[
            pl.BlockSpec((None, m, d), lambda i: (i, 0, 0)),
            _const_spec((1, d)),
            _const_spec((d, d)),
            _const_spec((d, d)),
        ],
        out_specs=[
            pl.BlockSpec((None, m, d), lambda i: (i, 0, 0)),
            pl.BlockSpec((None, m, d), lambda i: (i, 0, 0)),
        ],
        out_shape=[jax.ShapeDtypeStruct((b, m, d), BF16)] * 2,
        compiler_params=pltpu.CompilerParams(vmem_limit_bytes=VMEM_LIMIT),
        name="mem_kv",
    )(mem, mem_norm_w, wk, wv)


def _mixer_kernel(x_ref, ln1_ref, win_ref, wkt_ref, convw_ref, convb_ref, wa_ref, ba_ref, wx_ref,
                  bx_ref, lam_ref, bias_ref, gnr_ref, gna_ref, wout_ref, o_ref,
                  xr_ext, h_state, q_s, kt_s, vbuf, yatt):
    ts = SEQ_TILE
    i = pl.program_id(1)

    @pl.when(i == 0)
    def _():
        xr_ext[pl.ds(0, SUBLANES), :] = jnp.zeros((SUBLANES, D_RNN), F32)
        h_state[...] = jnp.zeros_like(h_state)
        kt_s[:, pl.ds(0, ts)] = jnp.zeros((D_ATT, ts), BF16)
        vbuf[pl.ds(0, ts), :] = jnp.zeros((ts, D_ATT), BF16)

    x = x_ref[...]
    h = _rms(x, ln1_ref[...]).astype(BF16)

    def proj(lo):
        return jnp.dot(h, win_ref[:, lo:lo + D_RNN], preferred_element_type=F32)

    xr_ext[pl.ds(SUBLANES, ts), :] = proj(0)
    xc = convb_ref[...] + sum(
        convw_ref[j:j + 1, :] * xr_ext[pl.ds(SUBLANES - (CONV_WIDTH - 1) + j, ts), :]
        for j in range(CONV_WIDTH))
    xr_ext[pl.ds(0, SUBLANES), :] = xr_ext[pl.ds(ts, SUBLANES), :]
    xcb = xc.astype(BF16)
    r = jax.nn.sigmoid(jnp.dot(xcb, wa_ref[...], preferred_element_type=F32) + ba_ref[...])
    gate_i = jax.nn.sigmoid(jnp.dot(xcb, wx_ref[...], preferred_element_type=F32) + bx_ref[...])
    z = -lam_ref[...]
    softplus = jnp.maximum(z, 0.0) + jnp.log1p(jnp.exp(-jnp.abs(z)))
    log_a = (-RG_LRU_C * r) * softplus
    a = jnp.exp(log_a)
    t = jnp.tanh(log_a)
    u = jnp.sqrt(-2.0 * t / (1.0 - t)) * (gate_i * xc)

    ng = ts // SUBLANES
    a3 = a.reshape(ng, SUBLANES, D_RNN)
    b3 = u.reshape(ng, SUBLANES, D_RNN)
    row = lax.broadcasted_iota(jnp.int32, (ng, SUBLANES, D_RNN), 1)
    for d in (1, 2, 4):
        a_sh = pltpu.roll(a3, d, 1)
        b_sh = pltpu.roll(b3, d, 1)
        m = row >= d
        b3 = jnp.where(m, a3 * b_sh + b3, b3)
        a3 = jnp.where(m, a3 * a_sh, a3)

    hprev = h_state[...]
    hs = []
    for g in range(ng):
        hg = a3[g] * hprev + b3[g]
        hs.append(hg)
        hprev = jnp.broadcast_to(hg[SUBLANES - 1:SUBLANES, :], (SUBLANES, D_RNN))
    h_state[...] = hprev
    hseq = jnp.concatenate(hs, axis=0)
    xg = proj(D_RNN)
    gelu = 0.5 * xg * (1.0 + jnp.tanh(0.7978845608028654 * (xg + 0.044715 * (xg * xg * xg))))
    y_rnn = _rms(gelu * hseq, gnr_ref[...]).astype(BF16)

    q_s[...] = (proj(2 * D_RNN) * (ATT_HEAD_DIM ** -0.5)).astype(BF16)
    kt_s[:, pl.ds(ts, ts)] = lax.dot_general(
        wkt_ref[...], h, (((1,), (1,)), ((), ())), preferred_element_type=F32).astype(BF16)
    vbuf[pl.ds(ts, ts), :] = proj(2 * D_RNN + 2 * D_ATT).astype(BF16)
    lane = lax.broadcasted_iota(jnp.int32, (PAIR_ROWS, PAIR_ROWS), 1)
    first_head = lane < ATT_HEAD_DIM
    key_pos = lax.broadcasted_iota(jnp.int32, (2 * PAIR_ROWS, PAIR_KEYS), 1)
    zero_q = jnp.zeros((PAIR_ROWS, PAIR_ROWS), BF16)

    for jj in range(ts // PAIR_ROWS):
        w0 = jj * PAIR_ROWS
        valid2 = key_pos >= jnp.where(i > 0, 0, ts - w0)
        scores = []
        for hp in range(N_ATT_HEADS // 2):
            fl = slice(hp * PAIR_ROWS, (hp + 1) * PAIR_ROWS)
            q2 = q_s[pl.ds(w0, PAIR_ROWS), fl]
            qm = jnp.concatenate(
                [jnp.where(first_head, q2, zero_q), jnp.where(first_head, zero_q, q2)], axis=0)
            scores.append(jnp.dot(qm, kt_s[fl, pl.ds(w0, PAIR_KEYS)], preferred_element_type=F32))
        outs = []
        for hp in range(N_ATT_HEADS // 2):
            fl = slice(hp * PAIR_ROWS, (hp + 1) * PAIR_ROWS)
            bias2 = jnp.concatenate([bias_ref[2 * hp], bias_ref[2 * hp + 1]], axis=0)
            s = jnp.where(valid2, scores[hp] + bias2, NEG_INF)
            p = jnp.exp(s - jnp.max(s, axis=-1, keepdims=True))
            l = jnp.sum(p, axis=-1, keepdims=True)
            o = jnp.dot(p.astype(BF16), vbuf[pl.ds(w0, PAIR_KEYS), fl],
                        preferred_element_type=F32) / l
            outs.append(jnp.where(first_head, o[:PAIR_ROWS], o[PAIR_ROWS:]))
        yatt[pl.ds(w0, PAIR_ROWS), :] = jnp.concatenate(outs, axis=-1)

    kt_s[:, pl.ds(0, ts)] = kt_s[:, pl.ds(ts, ts)]
    vbuf[pl.ds(0, ts), :] = vbuf[pl.ds(ts, ts), :]
    y_att = _rms(yatt[...], gna_ref[...]).astype(BF16)

    mix = (jnp.dot(y_rnn, wout_ref[pl.ds(0, D_RNN), :], preferred_element_type=F32)
           + jnp.dot(y_att, wout_ref[pl.ds(D_RNN, D_ATT), :], preferred_element_type=F32))
    o_ref[...] = x + mix


def _mixer_call(x, ln1_w, w_in, wk_t, conv_w, conv_b, wa_bd, ba, wx_bd, bx, lam, bias_band, gnr, gna,
                w_out):
    b, s, d = x.shape
    ts = SEQ_TILE
    assert s % ts == 0
    dp = w_in.shape[1]
    return pl.pallas_call(
        _mixer_kernel,
        grid=(b, s // ts),
        in_specs=[
            pl.BlockSpec((None, ts, d), lambda bi, i: (bi, i, 0)),
            _const_spec((1, d)),
            _const_spec((d, dp)),
            _const_spec((D_ATT, d)),
            _const_spec((CONV_WIDTH, D_RNN)),
            _const_spec((1, D_RNN)),
            _const_spec((D_RNN, D_RNN)),
            _const_spec((1, D_RNN)),
            _const_spec((D_RNN, D_RNN)),
            _const_spec((1, D_RNN)),
            _const_spec((1, D_RNN)),
            _const_spec((N_ATT_HEADS, PAIR_ROWS, PAIR_KEYS)),
            _const_spec((1, D_RNN)),
            _const_spec((1, D_ATT)),
            _const_spec((d, d)),
        ],
        out_specs=pl.BlockSpec((None, ts, d), lambda bi, i: (bi, i, 0)),
        out_shape=jax.ShapeDtypeStruct((b, s, d), F32),
        scratch_shapes=[
            pltpu.VMEM((ts + SUBLANES, D_RNN), F32),
            pltpu.VMEM((SUBLANES, D_RNN), F32),
            pltpu.VMEM((ts, D_ATT), BF16),
            pltpu.VMEM((D_ATT, 2 * ts), BF16),
            pltpu.VMEM((2 * ts, D_ATT), BF16),
            pltpu.VMEM((ts, D_ATT), F32),
        ],
        compiler_params=pltpu.CompilerParams(
            dimension_semantics=("arbitrary", "arbitrary"), vmem_limit_bytes=VMEM_LIMIT),
        name="mixer",
    )(x, ln1_w, w_in, wk_t, conv_w, conv_b, wa_bd, ba, wx_bd, bx, lam, bias_band, gnr, gna, w_out)


def _xattn_router_kernel(x_ref, ln2_ref, wq_ref, k_ref, v_ref, wo_ref, ln3_ref, wr_ref, br_ref,
                         x2_ref, h3_ref, eidx_ref, cw_ref):
    x = x_ref[...]
    h = _rms(x, ln2_ref[...]).astype(BF16)
    q = (jnp.dot(h, wq_ref[...], preferred_element_type=F32) * (XHEAD_DIM ** -0.5)).astype(BF16)
    outs = []
    for hd in range(N_XHEADS):
        sl = slice(hd * XHEAD_DIM, (hd + 1) * XHEAD_DIM)
        s = lax.dot_general(q[:, sl], k_ref[:, sl], (((1,), (1,)), ((), ())),
                            preferred_element_type=F32)
        p = jnp.exp(s - jnp.max(s, axis=-1, keepdims=True))
        l = jnp.sum(p, axis=-1, keepdims=True)
        o = jnp.dot(p.astype(BF16), v_ref[:, sl], preferred_element_type=F32)
        outs.append((o / l).astype(BF16))
    att = jnp.concatenate(outs, axis=-1)
    x2 = x + jnp.dot(att, wo_ref[...], preferred_element_type=F32)
    x2_ref[...] = x2

    h3 = _rms(x2, ln3_ref[...])
    h3_ref[...] = h3
    logits = lax.dot_general(wr_ref[...], h3.astype(BF16), (((1,), (1,)), ((), ())),
                             preferred_element_type=F32) + br_ref[...]
    gl = logits[0:N_GROUPS, :]
    ge = jnp.exp(gl - jnp.max(gl, axis=0, keepdims=True))
    gp = ge / jnp.sum(ge, axis=0, keepdims=True)
    g_p = jnp.max(gp, axis=0, keepdims=True)
    g_iota = lax.broadcasted_iota(jnp.int32, gp.shape, 0)
    g_idx = jnp.min(jnp.where(gp == g_p, g_iota, N_GROUPS), axis=0, keepdims=True)

    el = jnp.zeros((EXPERTS_PER_GROUP, logits.shape[1]), F32)
    for g in range(N_GROUPS):
        lo = SUBLANES + g * EXPERTS_PER_GROUP
        el = jnp.where(g_idx == g, logits[lo:lo + EXPERTS_PER_GROUP, :], el)
    ee = jnp.exp(el - jnp.max(el, axis=0, keepdims=True))
    ep = ee / jnp.sum(ee, axis=0, keepdims=True)
    e_iota = lax.broadcasted_iota(jnp.int32, ep.shape, 0)
    p1 = jnp.max(ep, axis=0, keepdims=True)
    i1 = jnp.min(jnp.where(ep == p1, e_iota, EXPERTS_PER_GROUP), axis=0, keepdims=True)
    ep2 = jnp.where(e_iota == i1, -1.0, ep)
    p2 = jnp.max(ep2, axis=0, keepdims=True)
    i2 = jnp.min(jnp.where(ep2 == p2, e_iota, EXPERTS_PER_GROUP), axis=0, keepdims=True)
    den = p1 + p2
    eidx_ref[...] = jnp.concatenate(
        [g_idx * EXPERTS_PER_GROUP + i1, g_idx * EXPERTS_PER_GROUP + i2], axis=0)
    cw_ref[...] = jnp.concatenate([g_p * (p1 / den), g_p * (p2 / den)], axis=0)


def _xattn_router_call(x1, ln2_w, wq, kmem, vmem, wo, ln3_w, wr_t, br_t):
    b, s, d = x1.shape
    tt = TOK_TILE
    m = kmem.shape[1]
    return pl.pallas_call(
        _xattn_router_kernel,
        grid=(b, s // tt),
        in_specs=[
            pl.BlockSpec((None, tt, d), lambda bi, i: (bi, i, 0)),
            _const_spec((1, d)),
            _const_spec((d, d)),
            pl.BlockSpec((None, m, d), lambda bi, i: (bi, 0, 0)),
            pl.BlockSpec((None, m, d), lambda bi, i: (bi, 0, 0)),
            _const_spec((d, d)),
            _const_spec((1, d)),
            _const_spec((ROUTER_ROWS, d)),
            _const_spec((ROUTER_ROWS, 1)),
        ],
        out_specs=[
            pl.BlockSpec((None, tt, d), lambda bi, i: (bi, i, 0)),
            pl.BlockSpec((None, tt, d), lambda bi, i: (bi, i, 0)),
            pl.BlockSpec((None, 2, tt), lambda bi, i: (bi, 0, i)),
            pl.BlockSpec((None, 2, tt), lambda bi, i: (bi, 0, i)),
        ],
        out_shape=[
            jax.ShapeDtypeStruct((b, s, d), F32),
            jax.ShapeDtypeStruct((b, s, d), F32),
            jax.ShapeDtypeStruct((b, 2, s), jnp.int32),
            jax.ShapeDtypeStruct((b, 2, s), F32),
        ],
        compiler_params=pltpu.CompilerParams(
            dimension_semantics=("arbitrary", "arbitrary"), vmem_limit_bytes=VMEM_LIMIT),
        name="xattn_router",
    )(x1, ln2_w, wq, kmem, vmem, wo, ln3_w, wr_t, br_t)


def _experts_kernel(be_ref, na_ref, src_ref, tgt_ref, h_ref, wg_ref, wu_ref, wd_ref, out_ref,
                    xbuf, ybuf, wg_s, wu_s, wd_s, prev_e, gsems, ssems):
    s = pl.program_id(0)
    na = na_ref[0]
    cur = s % 2

    def gather_copy(buf, k, src_row):
        return pltpu.make_async_copy(
            h_ref.at[pl.ds(src_row, 1), :], xbuf.at[buf, pl.ds(k, 1), :], gsems.at[buf])

    def scatter_copy(buf, k, dst_row):
        return pltpu.make_async_copy(
            ybuf.at[buf, pl.ds(k, 1), :], out_ref.at[pl.ds(dst_row, 1), :], ssems.at[buf])

    def gather(buf, table_row):
        for k in range(ROW_BLOCK):
            gather_copy(buf, k, src_ref[table_row, k]).start(priority=k % 2)

    def gather_wait(buf):
        for k in range(ROW_BLOCK):
            gather_copy(buf, k, 0).wait()

    def scatter(buf):
        for k in range(ROW_BLOCK):
            scatter_copy(buf, k, tgt_ref[0, k]).start(priority=(k + 1) % 2)

    def scatter_wait(buf):
        for k in range(ROW_BLOCK):
            scatter_copy(buf, k, 0).wait()

    def load_weights():
        e = be_ref[s]

        @pl.when(e != prev_e[0])
        def _():
            wg_s[...] = wg_ref[...].astype(BF16)
            wu_s[...] = wu_ref[...].astype(BF16)
            wd_s[...] = wd_ref[...].astype(BF16)
            prev_e[0] = e

    def compute(buf):
        xb = xbuf[buf].astype(BF16)
        g = jnp.dot(xb, wg_s[...], preferred_element_type=F32)
        u = jnp.dot(xb, wu_s[...], preferred_element_type=F32)
        hmid = (g * jax.nn.sigmoid(g) * u).astype(BF16)
        ybuf[buf] = jnp.dot(hmid, wd_s[...], preferred_element_type=F32)

    @pl.when(s == 0)
    def _():
        prev_e[0] = -1
        n_slots = out_ref.shape[0] - 2 * ROW_BLOCK
        ybuf[1] = jnp.zeros((ROW_BLOCK, ybuf.shape[2]), F32)
        spare = [pltpu.make_async_copy(
            ybuf.at[1], out_ref.at[pl.ds(n_slots + p * ROW_BLOCK, ROW_BLOCK), :], ssems.at[1])
            for p in range(2)]
        for cp in spare:
            cp.start()
        for cp in spare:
            cp.wait()
        gather(0, 0)

    @pl.when(jnp.logical_and(s >= 2, s - 2 < na))
    def _():
        scatter_wait(cur)

    @pl.when(s < na)
    def _():
        load_weights()
        gather_wait(cur)

    @pl.when(s == 0)
    def _():
        gather(1, 1)
        compute(0)

    @pl.when(jnp.logical_and(s >= 1, s < na))
    def _():
        scatter(1 - cur)
        gather(1 - cur, 1)
        compute(cur)

    @pl.when(s == na)
    def _():
        gather_wait(cur)
        scatter(1 - cur)


def _experts_call(block_expert, n_active, row_source, row_target, h3, w_gate, w_up, w_down,
                  n_out_rows):
    t, d = h3.shape
    nb = row_target.shape[0] // ROW_BLOCK
    de = w_gate.shape[-1]

    def blk(s, be, na):
        return jnp.minimum(s, na[0] - 1)

    def prev_blk(s, be, na):
        return jnp.clip(s - 1, 0, na[0] - 1)

    src_blocks = row_source.reshape(nb, ROW_BLOCK)
    src_pairs = jnp.stack([src_blocks, jnp.roll(src_blocks, -1, axis=0)], axis=1)
    grid_spec = pltpu.PrefetchScalarGridSpec(
        num_scalar_prefetch=2,
        grid=(nb + 2,),
        in_specs=[
            pl.BlockSpec((None, 2, ROW_BLOCK), lambda s, be, na: (blk(s, be, na), 0, 0),
                         memory_space=pltpu.SMEM),
            pl.BlockSpec((None, 1, ROW_BLOCK), lambda s, be, na: (prev_blk(s, be, na), 0, 0),
                         memory_space=pltpu.SMEM),
            pl.BlockSpec(memory_space=pl.ANY),
            pl.BlockSpec((None, d, de), lambda s, be, na: (be[blk(s, be, na)], 0, 0)),
            pl.BlockSpec((None, d, de), lambda s, be, na: (be[blk(s, be, na)], 0, 0)),
            pl.BlockSpec((None, de, d), lambda s, be, na: (be[blk(s, be, na)], 0, 0)),
        ],
        out_specs=pl.BlockSpec(memory_space=pl.ANY),
        scratch_shapes=[
            pltpu.VMEM((2, ROW_BLOCK, d), F32),
            pltpu.VMEM((2, ROW_BLOCK, d), F32),
            pltpu.VMEM((d, de), BF16),
            pltpu.VMEM((d, de), BF16),
            pltpu.VMEM((de, d), BF16),
            pltpu.SMEM((1,), jnp.int32),
            pltpu.SemaphoreType.DMA((2,)),
            pltpu.SemaphoreType.DMA((2,)),
        ],
    )
    return pl.pallas_call(
        _experts_kernel,
        grid_spec=grid_spec,
        out_shape=jax.ShapeDtypeStruct((n_out_rows, d), F32),
        compiler_params=pltpu.CompilerParams(
            dimension_semantics=("arbitrary",), vmem_limit_bytes=VMEM_LIMIT,
            has_side_effects=True),
        name="experts",
    )(block_expert, n_active, src_pairs, row_target.reshape(nb, 1, ROW_BLOCK), h3,
      w_gate, w_up, w_down)


def _combine_kernel(x_ref, y0_ref, y1_ref, cw_ref, fw_ref, o_ref):
    cw = cw_ref[...]
    moe = cw[:, 0:1] * y0_ref[...] + cw[:, 1:2] * y1_ref[...]
    o_ref[...] = _rms(x_ref[...] + moe, fw_ref[...])


def _combine_call(x2, y_slots, cw_t, final_w):
    t, d = x2.shape
    tm = TOK_TILE
    nt = t // tm
    return pl.pallas_call(
        _combine_kernel,
        grid=(nt,),
        in_specs=[
            pl.BlockSpec((tm, d), lambda i: (i, 0)),
            pl.BlockSpec((tm, d), lambda i: (i, 0)),
            pl.BlockSpec((tm, d), lambda i: (nt + i, 0)),
            pl.BlockSpec((tm, 2), lambda i: (i, 0)),
            _const_spec((1, d)),
        ],
        out_specs=pl.BlockSpec((tm, d), lambda i: (i, 0)),
        out_shape=jax.ShapeDtypeStruct((t, d), F32),
        compiler_params=pltpu.CompilerParams(
            dimension_semantics=("arbitrary",), vmem_limit_bytes=VMEM_LIMIT),
        name="combine",
    )(x2, y_slots, y_slots, cw_t, final_w)


def _block_diag(w):
    n, c, d = w.shape
    eye = jnp.eye(n, dtype=w.dtype)
    return (w[:, :, None, :] * eye[:, None, :, None]).reshape(n * c, n * d)


def _rel_bias_pair(rel_bias):
    nh = rel_bias.shape[0]
    far = LEFT_CHUNKS * CHUNK - REL_CLIP
    t = jnp.concatenate(
        [rel_bias[:, REL_CLIP - (CHUNK - 1):],
         jnp.broadcast_to(rel_bias[:, -1:], (nh, far + CHUNK - 1))], axis=1)
    period = t.shape[1] + 1
    tp = jnp.pad(t, ((0, 0), (0, 1)))
    hankel = jnp.tile(tp, (1, CHUNK + 1))[:, :CHUNK * (period + 1)].reshape(nh, CHUNK, period + 1)
    band = hankel[:, :, :BAND][:, :, ::-1]
    neg = jnp.full((nh, CHUNK, CHUNK), NEG_INF, F32)
    return jnp.concatenate(
        [jnp.concatenate([band, neg], axis=2), jnp.concatenate([neg, band], axis=2)], axis=1)


def _routing_plan(eidx, n_blocks):
    n_tok = eidx.shape[1]
    flat_e = eidx.reshape(-1)
    n_slots = flat_e.shape[0]
    order = jnp.argsort(flat_e).astype(jnp.int32)
    experts = jnp.arange(N_EXPERTS, dtype=jnp.int32)
    counts = jnp.sum((flat_e[:, None] == experts[None, :]).astype(jnp.int32), axis=0)
    seg_start = jnp.cumsum(counts) - counts
    padded = (counts + ROW_BLOCK - 1) // ROW_BLOCK * ROW_BLOCK
    pad_end = jnp.cumsum(padded)
    pad_start = pad_end - padded
    block_row = jnp.arange(n_blocks, dtype=jnp.int32) * ROW_BLOCK
    block_expert = jnp.minimum(
        jnp.sum((pad_end[None, :] <= block_row[:, None]).astype(jnp.int32), axis=1), N_EXPERTS - 1)
    n_active = (pad_end[-1] // ROW_BLOCK).astype(jnp.int32).reshape(1)

    in_block = jnp.arange(ROW_BLOCK, dtype=jnp.int32)[None, :]
    k = (block_row - pad_start[block_expert])[:, None] + in_block
    valid = k < counts[block_expert][:, None]
    sorted_pos = jnp.clip(seg_start[block_expert][:, None] + k, 0, n_slots - 1)
    slot = order[sorted_pos]
    spare = n_slots + (jnp.arange(n_blocks, dtype=jnp.int32) % 2)[:, None] * ROW_BLOCK + in_block
    row_target = jnp.where(valid, slot, spare).reshape(-1)
    row_source = jnp.where(valid, slot % n_tok, 0).reshape(-1)
    return block_expert, n_active, row_source, row_target


def kernel(x, mem, ln1_w, w_in, conv_w, conv_b, rnn_wa, rnn_ba, rnn_wx, rnn_bx, rnn_lambda, rel_bias, gn_rnn_w, gn_att_w, w_out, ln2_w, mem_norm_w, xq_w, xk_w, xv_w, xo_w, ln3_w, router_group_w, router_group_b, router_expert_w, router_expert_b, expert_gate_w, expert_up_w, expert_down_w, final_norm_w):
    b, s, d = x.shape
    t = b * s
    depth = ln1_w.shape[0]
    row = lambda v: v.reshape(1, -1)

    kmem, vmem = None, None
    for l in range(depth):
        x = _mixer_call(
            x, row(ln1_w[l]), w_in[l].astype(BF16),
            w_in[l][:, 2 * D_RNN + D_ATT:2 * D_RNN + 2 * D_ATT].T.astype(BF16),
            conv_w[l], row(conv_b[l]),
            _block_diag(rnn_wa[l]).astype(BF16), row(rnn_ba[l]),
            _block_diag(rnn_wx[l]).astype(BF16), row(rnn_bx[l]), row(rnn_lambda[l]),
            _rel_bias_pair(rel_bias[l]), row(gn_rnn_w[l]), row(gn_att_w[l]), w_out[l].astype(BF16))

        kmem, vmem = _memkv_call(mem, row(mem_norm_w), xk_w[l].astype(BF16), xv_w[l].astype(BF16))
        pad = jnp.zeros((SUBLANES - N_GROUPS, d), F32)
        wr_t = jnp.concatenate(
            [router_group_w[l].T, pad,
             router_expert_w[l].transpose(0, 2, 1).reshape(N_EXPERTS, d)], axis=0).astype(BF16)
        br_t = jnp.concatenate(
            [router_group_b[l], jnp.zeros((SUBLANES - N_GROUPS,), F32),
             router_expert_b[l].reshape(-1)]).reshape(ROUTER_ROWS, 1)
        x2, h3, eidx, cw = _xattn_router_call(
            x, row(ln2_w[l]), xq_w[l].astype(BF16), kmem, vmem, xo_w[l].astype(BF16),
            row(ln3_w[l]), wr_t, br_t)

        n_blocks = (2 * t) // ROW_BLOCK + N_EXPERTS
        eidx_t = eidx.transpose(1, 0, 2).reshape(2, t)
        block_expert, n_active, row_source, row_target = _routing_plan(eidx_t, n_blocks)
        y_slots = _experts_call(block_expert, n_active, row_source, row_target, h3.reshape(t, d),
                                expert_gate_w[l], expert_up_w[l], expert_down_w[l],
                                2 * t + 2 * ROW_BLOCK)
        cw_t = cw.transpose(0, 2, 1).reshape(t, 2)
        is_last = l == depth - 1
        assert is_last, "only a single layer is supported"
        x = _combine_call(x2.reshape(t, d), y_slots, cw_t, row(final_norm_w))
        x = x.reshape(b, s, d)
    return x
```

```python
import functools

import jax
import jax.numpy as jnp
import numpy as np
from jax import lax
from jax.experimental import pallas as pl
from jax.experimental.pallas import tpu as pltpu

F32 = jnp.float32
BF16 = jnp.bfloat16

D_MODEL = 1024
D_RNN = 512
D_ATT = 512
N_RNN_BLOCKS = 8
CONV_WIDTH = 4
RG_LRU_C = 8.0
N_ATT_HEADS = 8
ATT_HEAD_DIM = 64
CHUNK = 64
LEFT_CHUNKS = 8
BAND = (LEFT_CHUNKS + 1) * CHUNK
PAIR_ROWS = 2 * CHUNK
PAIR_KEYS = BAND + CHUNK
REL_CLIP = 128
N_XHEADS = 4
XHEAD_DIM = 256
N_GROUPS = 4
EXPERTS_PER_GROUP = 8
N_EXPERTS = 32
D_EXPERT = 512
EPS = 1e-6
NEG_INF = -1e30

SUBLANES = 8
SEQ_TILE = LEFT_CHUNKS * CHUNK
TOK_TILE = 512
ROW_BLOCK = 512
MOVE_TILE = 512
COMBINE_TILE = 1024
ROUTER_ROWS = 40
VMEM_LIMIT = 56 * 1024 * 1024


def _rms(x, w):
    return x * lax.rsqrt(jnp.mean(x * x, axis=-1, keepdims=True) + EPS) * w


def _const_spec(shape):
    return pl.BlockSpec(shape, lambda *_: (0,) * len(shape))


def _memkv_kernel(mem_ref, w_ref, wk_ref, wv_ref, k_ref, v_ref):
    mn = _rms(mem_ref[...], w_ref[...]).astype(BF16)
    k_ref[...] = jnp.dot(mn, wk_ref[...], preferred_element_type=F32).astype(BF16)
    v_ref[...] = jnp.dot(mn, wv_ref[...], preferred_element_type=F32).astype(BF16)


def _memkv_call(mem, mem_norm_w, wk, wv):
    b, m, d = mem.shape
    return pl.pallas_call(
        _memkv_kernel,
        grid=(b,),
        in_specs=[
            pl.BlockSpec((None, m, d), lambda i: (i, 0, 0)),
            _const_spec((1, d)),
            _const_spec((d, d)),
            _const_spec((d, d)),
        ],
        out_specs=[
            pl.BlockSpec((None, m, d), lambda i: (i, 0, 0)),
            pl.BlockSpec((None, m, d), lambda i: (i, 0, 0)),
        ],
        out_shape=[jax.ShapeDtypeStruct((b, m, d), BF16)] * 2,
        compiler_params=pltpu.CompilerParams(vmem_limit_bytes=VMEM_LIMIT),
        name="mem_kv",
    )(mem, mem_norm_w, wk, wv)


def _mixer_kernel(x_ref, ln1_ref, win_ref, wkt_ref, convw_ref, convb_ref, wa_ref, ba_ref, wx_ref,
                  bx_ref, lam_ref, bias_ref, gnr_ref, gna_ref, wout_ref, o_ref,
                  xr_ext, h_state, q_s, kt_s, vbuf, yatt):
    ts = SEQ_TILE
    i = pl.program_id(1)

    @pl.when(i == 0)
    def _():
        xr_ext[pl.ds(0, SUBLANES), :] = jnp.zeros((SUBLANES, D_RNN), F32)
        h_state[...] = jnp.zeros_like(h_state)
        kt_s[:, pl.ds(0, ts)] = jnp.zeros((D_ATT, ts), BF16)
        vbuf[pl.ds(0, ts), :] = jnp.zeros((ts, D_ATT), BF16)

    x = x_ref[...]
    h = _rms(x, ln1_ref[...]).astype(BF16)

    def proj(lo):
        return jnp.dot(h, win_ref[:, lo:lo + D_RNN], preferred_element_type=F32)

    xr_ext[pl.ds(SUBLANES, ts), :] = proj(0)
    xc = convb_ref[...] + sum(
        convw_ref[j:j + 1, :] * xr_ext[pl.ds(SUBLANES - (CONV_WIDTH - 1) + j, ts), :]
        for j in range(CONV_WIDTH))
    xr_ext[pl.ds(0, SUBLANES), :] = xr_ext[pl.ds(ts, SUBLANES), :]
    xcb = xc.astype(BF16)
    r = jax.nn.sigmoid(jnp.dot(xcb, wa_ref[...], preferred_element_type=F32) + ba_ref[...])
    gate_i = jax.nn.sigmoid(jnp.dot(xcb, wx_ref[...], preferred_element_type=F32) + bx_ref[...])
    z = -lam_ref[...]
    softplus = jnp.maximum(z, 0.0) + jnp.log1p(jnp.exp(-jnp.abs(z)))
    log_a = (-RG_LRU_C * r) * softplus
    a = jnp.exp(log_a)
    t = jnp.tanh(log_a)
    u = jnp.sqrt(-2.0 * t / (1.0 - t)) * (gate_i * xc)

    ng = ts // SUBLANES
    a3 = a.reshape(ng, SUBLANES, D_RNN)
    b3 = u.reshape(ng, SUBLANES, D_RNN)
    row = lax.broadcasted_iota(jnp.int32, (ng, SUBLANES, D_RNN), 1)
    for d in (1, 2, 4):
        a_sh = pltpu.roll(a3, d, 1)
        b_sh = pltpu.roll(b3, d, 1)
        m = row >= d
        b3 = jnp.where(m, a3 * b_sh + b3, b3)
        a3 = jnp.where(m, a3 * a_sh, a3)

    hprev = h_state[...]
    hs = []
    for g in range(ng):
        hg = a3[g] * hprev + b3[g]
        hs.append(hg)
        hprev = jnp.broadcast_to(hg[SUBLANES - 1:SUBLANES, :], (SUBLANES, D_RNN))
    h_state[...] = hprev
    hseq = jnp.concatenate(hs, axis=0)
    xg = proj(D_RNN)
    gelu = 0.5 * xg * (1.0 + jnp.tanh(0.7978845608028654 * (xg + 0.044715 * (xg * xg * xg))))
    y_rnn = _rms(gelu * hseq, gnr_ref[...]).astype(BF16)

    q_s[...] = (proj(2 * D_RNN) * (ATT_HEAD_DIM ** -0.5)).astype(BF16)
    kt_s[:, pl.ds(ts, ts)] = lax.dot_general(
        wkt_ref[...], h, (((1,), (1,)), ((), ())), preferred_element_type=F32).astype(BF16)
    vbuf[pl.ds(ts, ts), :] = proj(2 * D_RNN + 2 * D_ATT).astype(BF16)
    lane = lax.broadcasted_iota(jnp.int32, (PAIR_ROWS, PAIR_ROWS), 1)
    first_head = lane < ATT_HEAD_DIM
    key_pos = lax.broadcasted_iota(jnp.int32, (2 * PAIR_ROWS, PAIR_KEYS), 1)
    zero_q = jnp.zeros((PAIR_ROWS, PAIR_ROWS), BF16)

    for jj in range(ts // PAIR_ROWS):
        w0 = jj * PAIR_ROWS
        valid2 = key_pos >= jnp.where(i > 0, 0, ts - w0)
        scores = []
        for hp in range(N_ATT_HEADS // 2):
            fl = slice(hp * PAIR_ROWS, (hp + 1) * PAIR_ROWS)
            q2 = q_s[pl.ds(w0, PAIR_ROWS), fl]
            qm = jnp.concatenate(
                [jnp.where(first_head, q2, zero_q), jnp.where(first_head, zero_q, q2)], axis=0)
            scores.append(jnp.dot(qm, kt_s[fl, pl.ds(w0, PAIR_KEYS)], preferred_element_type=F32))
        outs = []
        for hp in range(N_ATT_HEADS // 2):
            fl = slice(hp * PAIR_ROWS, (hp + 1) * PAIR_ROWS)
            bias2 = jnp.concatenate([bias_ref[2 * hp], bias_ref[2 * hp + 1]], axis=0)
            s = jnp.where(valid2, scores[hp] + bias2, NEG_INF)
            p = jnp.exp(s - jnp.max(s, axis=-1, keepdims=True))
            l = jnp.sum(p, axis=-1, keepdims=True)
            o = jnp.dot(p.astype(BF16), vbuf[pl.ds(w0, PAIR_KEYS), fl],
                        preferred_element_type=F32) / l
            outs.append(jnp.where(first_head, o[:PAIR_ROWS], o[PAIR_ROWS:]))
        yatt[pl.ds(w0, PAIR_ROWS), :] = jnp.concatenate(outs, axis=-1)

    kt_s[:, pl.ds(0, ts)] = kt_s[:, pl.ds(ts, ts)]
    vbuf[pl.ds(0, ts), :] = vbuf[pl.ds(ts, ts), :]
    y_att = _rms(yatt[...], gna_ref[...]).astype(BF16)

    mix = (jnp.dot(y_rnn, wout_ref[pl.ds(0, D_RNN), :], preferred_element_type=F32)
           + jnp.dot(y_att, wout_ref[pl.ds(D_RNN, D_ATT), :], preferred_element_type=F32))
    o_ref[...] = x + mix


def _mixer_call(x, ln1_w, w_in, wk_t, conv_w, conv_b, wa_bd, ba, wx_bd, bx, lam, bias_band, gnr, gna,
                w_out):
    b, s, d = x.shape
    ts = SEQ_TILE
    assert s % ts == 0
    dp = w_in.shape[1]
    return pl.pallas_call(
        _mixer_kernel,
        grid=(b, s // ts),
        in_specs=[
            pl.BlockSpec((None, ts, d), lambda bi, i: (bi, i, 0)),
            _const_spec((1, d)),
            _const_spec((d, dp)),
            _const_spec((D_ATT, d)),
            _const_spec((CONV_WIDTH, D_RNN)),
            _const_spec((1, D_RNN)),
            _const_spec((D_RNN, D_RNN)),
            _const_spec((1, D_RNN)),
            _const_spec((D_RNN, D_RNN)),
            _const_spec((1, D_RNN)),
            _const_spec((1, D_RNN)),
            _const_spec((N_ATT_HEADS, PAIR_ROWS, PAIR_KEYS)),
            _const_spec((1, D_RNN)),
            _const_spec((1, D_ATT)),
            _const_spec((d, d)),
        ],
        out_specs=pl.BlockSpec((None, ts, d), lambda bi, i: (bi, i, 0)),
        out_shape=jax.ShapeDtypeStruct((b, s, d), F32),
        scratch_shapes=[
            pltpu.VMEM((ts + SUBLANES, D_RNN), F32),
            pltpu.VMEM((SUBLANES, D_RNN), F32),
            pltpu.VMEM((ts, D_ATT), BF16),
            pltpu.VMEM((D_ATT, 2 * ts), BF16),
            pltpu.VMEM((2 * ts, D_ATT), BF16),
            pltpu.VMEM((ts, D_ATT), F32),
        ],
        compiler_params=pltpu.CompilerParams(
            dimension_semantics=("arbitrary", "arbitrary"), vmem_limit_bytes=VMEM_LIMIT),
        name="mixer",
    )(x, ln1_w, w_in, wk_t, conv_w, conv_b, wa_bd, ba, wx_bd, bx, lam, bias_band, gnr, gna, w_out)


def _xattn_router_kernel(x_ref, ln2_ref, wq_ref, k_ref, v_ref, wo_ref, ln3_ref, wr_ref, br_ref,
                         x2_ref, h3_ref, eidx_ref, cw_ref):
    x = x_ref[...]
    h = _rms(x, ln2_ref[...]).astype(BF16)
    q = (jnp.dot(h, wq_ref[...], preferred_element_type=F32) * (XHEAD_DIM ** -0.5)).astype(BF16)
    outs = []
    for hd in range(N_XHEADS):
        sl = slice(hd * XHEAD_DIM, (hd + 1) * XHEAD_DIM)
        s = lax.dot_general(q[:, sl], k_ref[:, sl], (((1,), (1,)), ((), ())),
                            preferred_element_type=F32)
        p = jnp.exp(s - jnp.max(s, axis=-1, keepdims=True))
        l = jnp.sum(p, axis=-1, keepdims=True)
        o = jnp.dot(p.astype(BF16), v_ref[:, sl], preferred_element_type=F32)
        outs.append((o / l).astype(BF16))
    att = jnp.concatenate(outs, axis=-1)
    x2 = x + jnp.dot(att, wo_ref[...], preferred_element_type=F32)
    x2_ref[...] = x2

    h3 = _rms(x2, ln3_ref[...])
    h3_ref[...] = h3
    logits = lax.dot_general(wr_ref[...], h3.astype(BF16), (((1,), (1,)), ((), ())),
                             preferred_element_type=F32) + br_ref[...]
    gl = logits[0:N_GROUPS, :]
    ge = jnp.exp(gl - jnp.max(gl, axis=0, keepdims=True))
    gp = ge / jnp.sum(ge, axis=0, keepdims=True)
    g_p = jnp.max(gp, axis=0, keepdims=True)
    g_iota = lax.broadcasted_iota(jnp.int32, gp.shape, 0)
    g_idx = jnp.min(jnp.where(gp == g_p, g_iota, N_GROUPS), axis=0, keepdims=True)

    el = jnp.zeros((EXPERTS_PER_GROUP, logits.shape[1]), F32)
    for g in range(N_GROUPS):
        lo = SUBLANES + g * EXPERTS_PER_GROUP
        el = jnp.where(g_idx == g, logits[lo:lo + EXPERTS_PER_GROUP, :], el)
    ee = jnp.exp(el - jnp.max(el, axis=0, keepdims=True))
    ep = ee / jnp.sum(ee, axis=0, keepdims=True)
    e_iota = lax.broadcasted_iota(jnp.int32, ep.shape, 0)
    p1 = jnp.max(ep, axis=0, keepdims=True)
    i1 = jnp.min(jnp.where(ep == p1, e_iota, EXPERTS_PER_GROUP), axis=0, keepdims=True)
    ep2 = jnp.where(e_iota == i1, -1.0, ep)
    p2 = jnp.max(ep2, axis=0, keepdims=True)
    i2 = jnp.min(jnp.where(ep2 == p2, e_iota, EXPERTS_PER_GROUP), axis=0, keepdims=True)
    den = p1 + p2
    eidx_ref[...] = jnp.concatenate(
        [g_idx * EXPERTS_PER_GROUP + i1, g_idx * EXPERTS_PER_GROUP + i2], axis=0)
    cw_ref[...] = jnp.concatenate([g_p * (p1 / den), g_p * (p2 / den)], axis=0)


def _xattn_router_call(x1, ln2_w, wq, kmem, vmem, wo, ln3_w, wr_t, br_t):
    b, s, d = x1.shape
    tt = TOK_TILE
    m = kmem.shape[1]
    return pl.pallas_call(
        _xattn_router_kernel,
        grid=(b, s // tt),
        in_specs=[
            pl.BlockSpec((None, tt, d), lambda bi, i: (bi, i, 0)),
            _const_spec((1, d)),
            _const_spec((d, d)),
            pl.BlockSpec((None, m, d), lambda bi, i: (bi, 0, 0)),
            pl.BlockSpec((None, m, d), lambda bi, i: (bi, 0, 0)),
            _const_spec((d, d)),
            _const_spec((1, d)),
            _const_spec((ROUTER_ROWS, d)),
            _const_spec((ROUTER_ROWS, 1)),
        ],
        out_specs=[
            pl.BlockSpec((None, tt, d), lambda bi, i: (bi, i, 0)),
            pl.BlockSpec((None, tt, d), lambda bi, i: (bi, i, 0)),
            pl.BlockSpec((None, 2, tt), lambda bi, i: (bi, 0, i)),
            pl.BlockSpec((None, 2, tt), lambda bi, i: (bi, 0, i)),
        ],
        out_shape=[
            jax.ShapeDtypeStruct((b, s, d), F32),
            jax.ShapeDtypeStruct((b, s, d), F32),
            jax.ShapeDtypeStruct((b, 2, s), jnp.int32),
            jax.ShapeDtypeStruct((b, 2, s), F32),
        ],
        compiler_params=pltpu.CompilerParams(
            dimension_semantics=("arbitrary", "arbitrary"), vmem_limit_bytes=VMEM_LIMIT),
        name="xattn_router",
    )(x1, ln2_w, wq, kmem, vmem, wo, ln3_w, wr_t, br_t)


def _dispatch_kernel(tail_ref, na_ref, d0_ref, d1_ref, h_ref, spare_ref, xs_ref, tgt_ref,
                     zbuf, table, sem, zsem, tsem):
    i = pl.program_id(0)
    tm = MOVE_TILE
    nb = xs_ref.shape[0] // ROW_BLOCK
    n_tok = pl.num_programs(0) * tm

    @pl.when(i == 0)
    def _():
        cp = pltpu.make_async_copy(spare_ref, table, tsem)
        cp.start()
        cp.wait()

    @pl.when(i == 0)
    def _():
        zbuf[...] = jnp.zeros_like(zbuf)

        def zero_copy(start_row):
            start_row = pl.multiple_of(start_row, ROW_BLOCK)
            return pltpu.make_async_copy(zbuf, xs_ref.at[pl.ds(start_row, ROW_BLOCK), :], zsem)

        def start(e, c):
            @pl.when(tail_ref[e] >= 0)
            def _():
                zero_copy(tail_ref[e]).start()
            return c

        def wait(e, c):
            @pl.when(tail_ref[e] >= 0)
            def _():
                zero_copy(tail_ref[e]).wait()
            return c

        def start_unused(b, c):
            zero_copy(b * ROW_BLOCK).start()
            return c

        def wait_unused(b, c):
            zero_copy(b * ROW_BLOCK).wait()
            return c

        lax.fori_loop(0, N_EXPERTS, start, 0)
        lax.fori_loop(na_ref[0], nb, start_unused, 0)
        lax.fori_loop(0, N_EXPERTS, wait, 0)
        lax.fori_loop(na_ref[0], nb, wait_unused, 0)

    def row_copy(t, dst_row):
        return pltpu.make_async_copy(h_ref.at[pl.ds(t, 1), :], xs_ref.at[pl.ds(dst_row, 1), :], sem)

    for t in range(tm):
        r0 = d0_ref[0, t]
        r1 = d1_ref[0, t]
        row_copy(t, r0).start(priority=0)
        row_copy(t, r1).start(priority=1)
        table[r0] = i * tm + t
        table[r1] = n_tok + i * tm + t
    for t in range(tm):
        row_copy(t, 0).wait()
        row_copy(t, 0).wait()

    @pl.when(i == pl.num_programs(0) - 1)
    def _():
        cp = pltpu.make_async_copy(table, tgt_ref, tsem)
        cp.start()
        cp.wait()


def _dispatch_call(tail_start, n_active, dest0, dest1, h3, spare_target):
    t, d = h3.shape
    tm = MOVE_TILE
    nt = t // tm
    n_rows = spare_target.shape[0]
    grid_spec = pltpu.PrefetchScalarGridSpec(
        num_scalar_prefetch=2,
        grid=(nt,),
        in_specs=[
            pl.BlockSpec((None, 1, tm), lambda i, *_: (i, 0, 0), memory_space=pltpu.SMEM),
            pl.BlockSpec((None, 1, tm), lambda i, *_: (i, 0, 0), memory_space=pltpu.SMEM),
            pl.BlockSpec((tm, d), lambda i, *_: (i, 0)),
            pl.BlockSpec(memory_space=pl.ANY),
        ],
        out_specs=[pl.BlockSpec(memory_space=pl.ANY), pl.BlockSpec(memory_space=pl.ANY)],
        scratch_shapes=[
            pltpu.VMEM((ROW_BLOCK, d), F32),
            pltpu.SMEM((n_rows,), jnp.int32),
            pltpu.SemaphoreType.DMA(()),
            pltpu.SemaphoreType.DMA(()),
            pltpu.SemaphoreType.DMA(()),
        ],
    )
    return pl.pallas_call(
        _dispatch_kernel,
        grid_spec=grid_spec,
        out_shape=[jax.ShapeDtypeStruct((n_rows, d), F32),
                   jax.ShapeDtypeStruct((n_rows,), jnp.int32)],
        compiler_params=pltpu.CompilerParams(
            dimension_semantics=("arbitrary",), vmem_limit_bytes=VMEM_LIMIT,
            has_side_effects=True),
        name="dispatch",
    )(tail_start, n_active, dest0.reshape(nt, 1, tm), dest1.reshape(nt, 1, tm), h3, spare_target)


def _experts_kernel(be_ref, na_ref, tgt_ref, xs_ref, wg_ref, wu_ref, wd_ref, out_ref,
                    ybuf, wg_s, wu_s, wd_s, prev_e, sems):
    s = pl.program_id(0)
    na = na_ref[0]
    cur = s % 2

    @pl.when(s == 0)
    def _():
        prev_e[0] = -1

    def row_copy(buf, k, dst_row):
        return pltpu.make_async_copy(
            ybuf.at[buf, pl.ds(k, 1), :], out_ref.at[pl.ds(dst_row, 1), :], sems.at[buf])

    def drain(buf):
        for k in range(ROW_BLOCK):
            row_copy(buf, k, 0).wait()

    def scatter(buf):
        for k in range(ROW_BLOCK):
            row_copy(buf, k, tgt_ref[0, k]).start(priority=k % 2)

    def load_weights():
        e = be_ref[s]

        @pl.when(e != prev_e[0])
        def _():
            wg_s[...] = wg_ref[...].astype(BF16)
            wu_s[...] = wu_ref[...].astype(BF16)
            wd_s[...] = wd_ref[...].astype(BF16)
            prev_e[0] = e

    def compute(buf):
        xb = xs_ref[...].astype(BF16)
        g = jnp.dot(xb, wg_s[...], preferred_element_type=F32)
        u = jnp.dot(xb, wu_s[...], preferred_element_type=F32)
        hmid = (g * jax.nn.sigmoid(g) * u).astype(BF16)
        ybuf[buf] = jnp.dot(hmid, wd_s[...], preferred_element_type=F32)

    @pl.when(jnp.logical_and(s >= 2, s - 2 < na))
    def _():
        drain(cur)

    @pl.when(s < na)
    def _():
        load_weights()

    @pl.when(s == 0)
    def _():
        n_slots = out_ref.shape[0] - 2 * ROW_BLOCK
        ybuf[1] = jnp.zeros((ROW_BLOCK, ybuf.shape[2]), F32)
        spare = [pltpu.make_async_copy(
            ybuf.at[1], out_ref.at[pl.ds(n_slots + p * ROW_BLOCK, ROW_BLOCK), :], sems.at[1])
            for p in range(2)]
        for cp in spare:
            cp.start()
        for cp in spare:
            cp.wait()
        compute(cur)

    @pl.when(jnp.logical_and(s >= 1, s < na))
    def _():
        scatter(1 - cur)
        compute(cur)

    @pl.when(s == na)
    def _():
        scatter(1 - cur)


def _experts_call(block_expert, n_active, row_target, xs, w_gate, w_up, w_down, n_out_rows):
    n_rows, d = xs.shape
    nb = n_rows // ROW_BLOCK
    de = w_gate.shape[-1]

    def blk(s, be, na):
        return jnp.minimum(s, na[0] - 1)

    def prev_blk(s, be, na):
        return jnp.clip(s - 1, 0, na[0] - 1)

    grid_spec = pltpu.PrefetchScalarGridSpec(
        num_scalar_prefetch=2,
        grid=(nb + 2,),
        in_specs=[
            pl.BlockSpec((None, 1, ROW_BLOCK), lambda s, be, na: (prev_blk(s, be, na), 0, 0),
                         memory_space=pltpu.SMEM),
            pl.BlockSpec((ROW_BLOCK, d), lambda s, be, na: (blk(s, be, na), 0)),
            pl.BlockSpec((None, d, de), lambda s, be, na: (be[blk(s, be, na)], 0, 0)),
            pl.BlockSpec((None, d, de), lambda s, be, na: (be[blk(s, be, na)], 0, 0)),
            pl.BlockSpec((None, de, d), lambda s, be, na: (be[blk(s, be, na)], 0, 0)),
        ],
        out_specs=pl.BlockSpec(memory_space=pl.ANY),
        scratch_shapes=[
            pltpu.VMEM((2, ROW_BLOCK, d), F32),
            pltpu.VMEM((d, de), BF16),
            pltpu.VMEM((d, de), BF16),
            pltpu.VMEM((de, d), BF16),
            pltpu.SMEM((1,), jnp.int32),
            pltpu.SemaphoreType.DMA((2,)),
        ],
    )
    return pl.pallas_call(
        _experts_kernel,
        grid_spec=grid_spec,
        out_shape=jax.ShapeDtypeStruct((n_out_rows, d), F32),
        compiler_params=pltpu.CompilerParams(
            dimension_semantics=("arbitrary",), vmem_limit_bytes=VMEM_LIMIT,
            has_side_effects=True),
        name="experts",
    )(block_expert, n_active, row_target.reshape(nb, 1, ROW_BLOCK), xs, w_gate, w_up, w_down)


def _combine_kernel(x_ref, y0_ref, y1_ref, cw_ref, fw_ref, o_ref):
    cw = cw_ref[...]
    moe = cw[:, 0:1] * y0_ref[...] + cw[:, 1:2] * y1_ref[...]
    o_ref[...] = _rms(x_ref[...] + moe, fw_ref[...])


def _combine_call(x2, y_slots, cw_t, final_w):
    t, d = x2.shape
    tm = COMBINE_TILE
    nt = t // tm
    return pl.pallas_call(
        _combine_kernel,
        grid=(nt,),
        in_specs=[
            pl.BlockSpec((tm, d), lambda i: (i, 0)),
            pl.BlockSpec((tm, d), lambda i: (i, 0)),
            pl.BlockSpec((tm, d), lambda i: (nt + i, 0)),
            pl.BlockSpec((tm, 2), lambda i: (i, 0)),
            _const_spec((1, d)),
        ],
        out_specs=pl.BlockSpec((tm, d), lambda i: (i, 0)),
        out_shape=jax.ShapeDtypeStruct((t, d), F32),
        compiler_params=pltpu.CompilerParams(
            dimension_semantics=("arbitrary",), vmem_limit_bytes=VMEM_LIMIT),
        name="combine",
    )(x2, y_slots, y_slots, cw_t, final_w)


def _block_diag(w):
    n, c, d = w.shape
    eye = jnp.eye(n, dtype=w.dtype)
    return (w[:, :, None, :] * eye[:, None, :, None]).reshape(n * c, n * d)


def _rel_bias_pair(rel_bias):
    nh = rel_bias.shape[0]
    far = LEFT_CHUNKS * CHUNK - REL_CLIP
    t = jnp.concatenate(
        [rel_bias[:, REL_CLIP - (CHUNK - 1):],
         jnp.broadcast_to(rel_bias[:, -1:], (nh, far + CHUNK - 1))], axis=1)
    period = t.shape[1] + 1
    tp = jnp.pad(t, ((0, 0), (0, 1)))
    hankel = jnp.tile(tp, (1, CHUNK + 1))[:, :CHUNK * (period + 1)].reshape(nh, CHUNK, period + 1)
    band = hankel[:, :, :BAND][:, :, ::-1]
    neg = jnp.full((nh, CHUNK, CHUNK), NEG_INF, F32)
    return jnp.concatenate(
        [jnp.concatenate([band, neg], axis=2), jnp.concatenate([neg, band], axis=2)], axis=1)


def _routing_plan(eidx, n_blocks):
    flat_e = eidx.reshape(-1)
    onehot = (flat_e[:, None] == jnp.arange(N_EXPERTS, dtype=jnp.int32)[None, :]).astype(jnp.int32)
    csum = jnp.cumsum(onehot, axis=0)
    counts = csum[-1]
    rank = jnp.sum(onehot * csum, axis=1) - 1
    padded = (counts + ROW_BLOCK - 1) // ROW_BLOCK * ROW_BLOCK
    pad_end = jnp.cumsum(padded)
    pad_start = pad_end - padded
    dest = jnp.sum(onehot * pad_start[None, :], axis=1) + rank
    block_row = jnp.arange(n_blocks, dtype=jnp.int32) * ROW_BLOCK
    block_expert = jnp.minimum(
        jnp.sum((pad_end[None, :] <= block_row[:, None]).astype(jnp.int32), axis=1), N_EXPERTS - 1)
    n_active = (pad_end[-1] // ROW_BLOCK).astype(jnp.int32).reshape(1)
    tail_start = jnp.where(padded > 0, pad_end - ROW_BLOCK, -1).astype(jnp.int32)
    dest = dest.astype(jnp.int32)
    n_slots = flat_e.shape[0]
    rows = jnp.arange(n_blocks * ROW_BLOCK, dtype=jnp.int32)
    spare_target = n_slots + (rows // ROW_BLOCK % 2) * ROW_BLOCK + rows % ROW_BLOCK
    return dest.reshape(eidx.shape), block_expert, n_active, tail_start, spare_target


def kernel(x, mem, ln1_w, w_in, conv_w, conv_b, rnn_wa, rnn_ba, rnn_wx, rnn_bx, rnn_lambda, rel_bias, gn_rnn_w, gn_att_w, w_out, ln2_w, mem_norm_w, xq_w, xk_w, xv_w, xo_w, ln3_w, router_group_w, router_group_b, router_expert_w, router_expert_b, expert_gate_w, expert_up_w, expert_down_w, final_norm_w):
    b, s, d = x.shape
    t = b * s
    depth = ln1_w.shape[0]
    row = lambda v: v.reshape(1, -1)

    kmem, vmem = None, None
    for l in range(depth):
        x = _mixer_call(
            x, row(ln1_w[l]), w_in[l].astype(BF16),
            w_in[l][:, 2 * D_RNN + D_ATT:2 * D_RNN + 2 * D_ATT].T.astype(BF16),
            conv_w[l], row(conv_b[l]),
            _block_diag(rnn_wa[l]).astype(BF16), row(rnn_ba[l]),
            _block_diag(rnn_wx[l]).astype(BF16), row(rnn_bx[l]), row(rnn_lambda[l]),
            _rel_bias_pair(rel_bias[l]), row(gn_rnn_w[l]), row(gn_att_w[l]), w_out[l].astype(BF16))

        kmem, vmem = _memkv_call(mem, row(mem_norm_w), xk_w[l].astype(BF16), xv_w[l].astype(BF16))
        pad = jnp.zeros((SUBLANES - N_GROUPS, d), F32)
        wr_t = jnp.concatenate(
            [router_group_w[l].T, pad,
             router_expert_w[l].transpose(0, 2, 1).reshape(N_EXPERTS, d)], axis=0).astype(BF16)
        br_t = jnp.concatenate(
            [router_group_b[l], jnp.zeros((SUBLANES - N_GROUPS,), F32),
             router_expert_b[l].reshape(-1)]).reshape(ROUTER_ROWS, 1)
        x2, h3, eidx, cw = _xattn_router_call(
            x, row(ln2_w[l]), xq_w[l].astype(BF16), kmem, vmem, xo_w[l].astype(BF16),
            row(ln3_w[l]), wr_t, br_t)

        n_blocks = (2 * t) // ROW_BLOCK + N_EXPERTS
        eidx_t = eidx.transpose(1, 0, 2).reshape(2, t)
        dest, block_expert, n_active, tail_start, spare_target = _routing_plan(eidx_t, n_blocks)
        xs, row_target = _dispatch_call(tail_start, n_active, dest[0], dest[1], h3.reshape(t, d),
                                        spare_target)
        y_slots = _experts_call(block_expert, n_active, row_target, xs, expert_gate_w[l],
                                expert_up_w[l], expert_down_w[l], 2 * t + 2 * ROW_BLOCK)
        cw_t = cw.transpose(0, 2, 1).reshape(t, 2)
        is_last = l == depth - 1
        assert is_last, "only a single layer is supported"
        x = _combine_call(x2.reshape(t, d), y_slots, cw_t, row(final_norm_w))
        x = x.reshape(b, s, d)
    return x
```
